```python
import jax, jax.numpy as jnp
from jax import lax
import numpy as np

D_MODEL = 1024
BATCH = 4
SEQ = 8192
DEPTH = 2
DEC_BATCH = 32
DEC_SEQ = 8
PAST_LEN = 16384
PAGE_SIZE = 128

N_GROUPS = 4
W_GROUP = D_MODEL // N_GROUPS
MIX_WIDTH = N_GROUPS * W_GROUP
HEAD_DIM = 64
H_ATT = W_GROUP // HEAD_DIM
H_MEM = 4
MEM_DIM = W_GROUP // H_MEM
N_MEM = 256
CONV_W = 3
POOL_WINDOWS = (2, 4, 8, 16)
N_POOL_GROUPS = len(POOL_WINDOWS)
POOL_GC = W_GROUP // N_POOL_GROUPS
POOL_STATE = max(POOL_WINDOWS) - 1
MOBA_BLOCK = 256
MOBA_TOPK = 3
Q_CHUNK = 64
N_IN_SPLITS = 12
IN_COLS = N_IN_SPLITS * W_GROUP
EPS = 1e-6

kernel_name = "hymba_conv_pool_moba_mem_step"


def rmsnorm(x, g):
    x32 = x.astype(jnp.float32)
    y = x32 * lax.rsqrt(jnp.mean(x32 * x32, axis=-1, keepdims=True) + EPS)
    return y.astype(x.dtype) * g


def alibi_slopes():
    return jnp.asarray([2.0 ** (-8.0 * (h + 1) / H_ATT) for h in range(H_ATT)], dtype=jnp.float32)


def short_conv(u, prev, w, b):
    t = u.shape[1]
    ext = jnp.concatenate([prev, u], axis=1)
    y = b
    for j in range(CONV_W):
        y = y + w[j] * ext[:, j:j + t]
    return y, ext[:, -(CONV_W - 1):]


def multiscale_pool(v, prev, start, w_pool, scale):
    b, t, _ = v.shape
    ext = jnp.concatenate([prev, v], axis=1)
    cs = jnp.cumsum(ext.astype(jnp.float32), axis=1)
    cs = jnp.concatenate([jnp.zeros((b, 1, W_GROUP), jnp.float32), cs], axis=1)
    end = cs[:, POOL_STATE + 1:]
    pos = start + jnp.arange(t, dtype=jnp.int32)
    means = []
    for g, w in enumerate(POOL_WINDOWS):
        sl = slice(g * POOL_GC, (g + 1) * POOL_GC)
        begin = cs[:, POOL_STATE + 1 - w: POOL_STATE + 1 - w + t, sl]
        cnt = jnp.minimum(pos + 1, w).astype(jnp.float32)[None, :, None]
        means.append((end[..., sl] - begin) / cnt)
    mean = jnp.concatenate(means, axis=-1).astype(v.dtype)
    pooled = (mean - v).reshape(b, t, N_POOL_GROUPS, POOL_GC)
    y = jnp.einsum('btgc,gcd->btgd', pooled, w_pool).reshape(b, t, W_GROUP) * scale
    return y, ext[:, -POOL_STATE:]


def moba_blocks(k_all, v_all):
    b, l, h, d = k_all.shape
    nb = max(-(-l // MOBA_BLOCK), MOBA_TOPK)
    padw = ((0, 0), (0, nb * MOBA_BLOCK - l), (0, 0), (0, 0))
    kb = jnp.pad(k_all, padw).reshape(b, nb, MOBA_BLOCK, h, d).transpose(0, 3, 1, 2, 4)
    vb = jnp.pad(v_all, padw).reshape(b, nb, MOBA_BLOCK, h, d).transpose(0, 3, 1, 2, 4)
    kmean = jnp.mean(kb.astype(jnp.float32), axis=3).astype(k_all.dtype)
    return kb, vb, kmean


def moba_attend(q, qpos, kb, vb, kmean):
    b, tq, h, d = q.shape
    nb = kb.shape[2]
    slopes = alibi_slopes()
    qh = q.transpose(0, 2, 1, 3)
    qblk = qpos // MOBA_BLOCK
    gate = jnp.einsum('bhtd,bhnd->bhtn', qh, kmean).astype(jnp.float32)
    fully_past = jnp.arange(nb)[None, :] < qblk[:, None]
    gate = jnp.where(fully_past[None, None], gate, -jnp.inf)
    _, idx = lax.top_k(gate, MOBA_TOPK)
    valid = idx < qblk[None, None, :, None]
    bi = jnp.arange(b)[:, None, None, None]
    hi = jnp.arange(h)[None, :, None, None]
    ksel = kb[bi, hi, idx]
    vsel = vb[bi, hi, idx]
    kown = kb[:, :, qblk]
    vown = vb[:, :, qblk]
    j = jnp.arange(MOBA_BLOCK, dtype=jnp.int32)
    dist_sel = (qpos[None, None, :, None, None] - (idx[..., None] * MOBA_BLOCK + j)).astype(jnp.float32)
    s_sel = jnp.einsum('bhtd,bhtkjd->bhtkj', qh, ksel).astype(jnp.float32) - slopes[None, :, None, None, None] * dist_sel
    s_sel = jnp.where(valid[..., None], s_sel, -jnp.inf).reshape(b, h, tq, MOBA_TOPK * MOBA_BLOCK)
    dist_own = qpos[:, None] - (qblk[:, None] * MOBA_BLOCK + j[None, :])
    s_own = jnp.einsum('bhtd,bhtjd->bhtj', qh, kown).astype(jnp.float32) - slopes[None, :, None, None] * dist_own.astype(jnp.float32)[None, None]
    s_own = jnp.where((dist_own >= 0)[None, None], s_own, -jnp.inf)
    p = jax.nn.softmax(jnp.concatenate([s_sel, s_own], axis=-1), axis=-1).astype(vb.dtype)
    p_sel = p[..., :MOBA_TOPK * MOBA_BLOCK].reshape(b, h, tq, MOBA_TOPK, MOBA_BLOCK)
    out = jnp.einsum('bhtkj,bhtkjd->bhtd', p_sel, vsel) + jnp.einsum('bhtj,bhtjd->bhtd', p[..., MOBA_TOPK * MOBA_BLOCK:], vown)
    return out.transpose(0, 2, 1, 3)


def memory_kv(mem, g_mem, w_mem_kv, g_mk):
    b, n, _ = mem.shape
    mk, mv = jnp.split(rmsnorm(mem, g_mem) @ w_mem_kv, 2, axis=-1)
    return rmsnorm(mk.reshape(b, n, H_MEM, MEM_DIM), g_mk), mv.reshape(b, n, H_MEM, MEM_DIM)


def hybrid_layer(x, k_past, v_past, conv_prev, pool_prev, mem_k, mem_v,
                 g_norm, w_in, w_out, conv_w, conv_b, pool_w, pool_scale, g_q, g_k, g_mq):
    b, t, _ = x.shape
    start = k_past.shape[1]
    (c_h, c_b, c_c, c_z, p_v, p_z, a_q, a_k, a_v, a_z, m_q, m_z) = jnp.split(rmsnorm(x, g_norm) @ w_in, N_IN_SPLITS, axis=-1)
    conv_out, conv_new = short_conv(c_c * c_h, conv_prev, conv_w, conv_b)
    y_conv = c_b * conv_out * jax.nn.silu(c_z)
    pool_out, pool_new = multiscale_pool(p_v, pool_prev, start, pool_w, pool_scale)
    y_pool = pool_out * jax.nn.silu(p_z)
    q = rmsnorm(a_q.reshape(b, t, H_ATT, HEAD_DIM), g_q) * (HEAD_DIM ** -0.5)
    k = rmsnorm(a_k.reshape(b, t, H_ATT, HEAD_DIM), g_k)
    v = a_v.reshape(b, t, H_ATT, HEAD_DIM)
    kb, vb, kmean = moba_blocks(jnp.concatenate([k_past, k], axis=1), jnp.concatenate([v_past, v], axis=1))
    qpos = start + jnp.arange(t, dtype=jnp.int32)
    if t > Q_CHUNK and t % Q_CHUNK == 0:
        nc = t // Q_CHUNK
        qc = q.reshape(b, nc, Q_CHUNK, H_ATT, HEAD_DIM).transpose(1, 0, 2, 3, 4)
        pc = qpos.reshape(nc, Q_CHUNK)
        att = lax.map(lambda a: moba_attend(a[0], a[1], kb, vb, kmean), (qc, pc))
        att = att.transpose(1, 0, 2, 3, 4).reshape(b, t, W_GROUP)
    else:
        att = moba_attend(q, qpos, kb, vb, kmean).reshape(b, t, W_GROUP)
    y_att = att * jax.nn.silu(a_z)
    mq = rmsnorm(m_q.reshape(b, t, H_MEM, MEM_DIM), g_mq) * (MEM_DIM ** -0.5)
    s = jnp.einsum('bthd,bmhd->bhtm', mq, mem_k).astype(jnp.float32)
    pm = jax.nn.softmax(s, axis=-1).astype(mem_v.dtype)
    y_mem = jnp.einsum('bhtm,bmhd->bthd', pm, mem_v).reshape(b, t, W_GROUP) * jax.nn.silu(m_z)
    y = jnp.concatenate([y_conv, y_pool, y_att, y_mem], axis=-1) @ w_out
    return x + y, k, v, conv_new, pool_new


def setup_inputs(seed: int = 0) -> dict:
    key = jax.random.key(seed)
    ks = jax.random.split(key, 24)
    n_pages = PAST_LEN // PAGE_SIZE
    n_phys = (DEC_BATCH * n_pages * 5) // 4
    nrm = lambda k, shape, s: jax.random.normal(k, shape, jnp.float32) * s
    gain = lambda k, shape: 1.0 + 0.1 * jax.random.normal(k, shape, jnp.float32)
    perm = jax.random.permutation(ks[0], n_phys)
    page_table = perm[:DEC_BATCH * n_pages].reshape(DEC_BATCH, n_pages).astype(jnp.int32)
    return {
        "x_prompt": nrm(ks[1], (BATCH, SEQ, D_MODEL), 1.0),
        "x_sample": nrm(ks[2], (DEC_BATCH, DEC_SEQ, D_MODEL), 1.0),
        "cache_k": nrm(ks[3], (DEPTH, n_phys, PAGE_SIZE, H_ATT, HEAD_DIM), 1.0),
        "cache_v": nrm(ks[4], (DEPTH, n_phys, PAGE_SIZE, H_ATT, HEAD_DIM), 1.0),
        "page_table": page_table,
        "state_conv": nrm(ks[5], (DEPTH, DEC_BATCH, CONV_W - 1, W_GROUP), 1.0),
        "state_pool": nrm(ks[6], (DEPTH, DEC_BATCH, POOL_STATE, W_GROUP), 1.0),
        "cache_mem_k": nrm(ks[7], (DEPTH, DEC_BATCH, N_MEM, H_MEM, MEM_DIM), 1.0),
        "cache_mem_v": nrm(ks[8], (DEPTH, DEC_BATCH, N_MEM, H_MEM, MEM_DIM), 1.0),
        "mem_prompt": nrm(ks[9], (BATCH, N_MEM, D_MODEL), 1.0),
        "g_norm": gain(ks[10], (DEPTH, D_MODEL)),
        "w_in": nrm(ks[11], (DEPTH, D_MODEL, IN_COLS), D_MODEL ** -0.5),
        "w_out": nrm(ks[12], (DEPTH, MIX_WIDTH, D_MODEL), MIX_WIDTH ** -0.5),
        "conv_w": nrm(ks[13], (DEPTH, CONV_W, W_GROUP), CONV_W ** -0.5),
        "conv_b": nrm(ks[14], (DEPTH, W_GROUP), 0.01),
        "pool_w": nrm(ks[15], (DEPTH, N_POOL_GROUPS, POOL_GC, POOL_GC), POOL_GC ** -0.5),
        "pool_scale": gain(ks[16], (DEPTH, W_GROUP)),
        "g_q": gain(ks[17], (DEPTH, HEAD_DIM)),
        "g_k": gain(ks[18], (DEPTH, HEAD_DIM)),
        "g_mq": gain(ks[19], (DEPTH, MEM_DIM)),
        "g_mk": gain(ks[20], (DEPTH, MEM_DIM)),
        "g_mem": gain(ks[21], (DEPTH, D_MODEL)),
        "w_mem_kv": nrm(ks[22], (DEPTH, D_MODEL, 2 * W_GROUP), D_MODEL ** -0.5),
    }


def reference(x_prompt, x_sample, cache_k, cache_v, page_table, state_conv, state_pool,
              cache_mem_k, cache_mem_v, mem_prompt, g_norm, w_in, w_out, conv_w, conv_b,
              pool_w, pool_scale, g_q, g_k, g_mq, g_mk, g_mem, w_mem_kv):
    n_pages = page_table.shape[1]
    bp, bs = x_prompt.shape[0], x_sample.shape[0]
    xp, xs = x_prompt, x_sample
    zero_kv = jnp.zeros((bp, 0, H_ATT, HEAD_DIM), x_prompt.dtype)
    zero_conv = jnp.zeros((bp, CONV_W - 1, W_GROUP), x_prompt.dtype)
    zero_pool = jnp.zeros((bp, POOL_STATE, W_GROUP), x_prompt.dtype)
    kp_l, vp_l, ks_l, vs_l, cp_l, cs_l, pp_l, ps_l, mk_l, mv_l = ([] for _ in range(10))
    for l in range(DEPTH):
        w = (g_norm[l], w_in[l], w_out[l], conv_w[l], conv_b[l], pool_w[l], pool_scale[l], g_q[l], g_k[l], g_mq[l])
        mk, mv = memory_kv(mem_prompt, g_mem[l], w_mem_kv[l], g_mk[l])
        xp, kp, vp, cp, pp = hybrid_layer(xp, zero_kv, zero_kv, zero_conv, zero_pool, mk, mv, *w)
        k_past = cache_k[l][page_table].reshape(bs, n_pages * PAGE_SIZE, H_ATT, HEAD_DIM)
        v_past = cache_v[l][page_table].reshape(bs, n_pages * PAGE_SIZE, H_ATT, HEAD_DIM)
        xs, ksn, vsn, csn, psn = hybrid_layer(xs, k_past, v_past, state_conv[l], state_pool[l],
                                               cache_mem_k[l], cache_mem_v[l], *w)
        kp_l.append(kp); vp_l.append(vp); ks_l.append(ksn); vs_l.append(vsn)
        cp_l.append(cp); cs_l.append(csn); pp_l.append(pp); ps_l.append(psn)
        mk_l.append(mk); mv_l.append(mv)
    return (xp, xs, jnp.stack(kp_l), jnp.stack(vp_l), jnp.stack(ks_l), jnp.stack(vs_l),
            jnp.stack(cp_l), jnp.stack(cs_l), jnp.stack(pp_l), jnp.stack(ps_l),
            jnp.stack(mk_l), jnp.stack(mv_l))
```

```python
import functools

import jax
import jax.numpy as jnp
from jax import lax
from jax.experimental import pallas as pl
from jax.experimental.pallas import tpu as pltpu

F32 = jnp.float32
BF16 = jnp.bfloat16

D_MODEL = 1024
W_GROUP = 256
N_HEADS = 4
HEAD_DIM = 64
N_MEM = 256
CONV_W = 3
POOL_WINDOWS = (2, 4, 8, 16)
POOL_STATE = 15
MOBA_BLOCK = 256
MOBA_TOPK = 3
PAGE_SIZE = 128
N_IN_SPLITS = 12
EPS = 1e-6
NEG_INF = float("-inf")

ROW_TILE = 256
POOL_HALO = 32
CONV_HALO = 8
PAGES_PER_STEP = 16
PAGES_PER_BLOCK = MOBA_BLOCK // PAGE_SIZE
VMEM_LIMIT = 56 * 1024 * 1024


def _slope(h):
    return 2.0 ** (-8.0 * (h + 1) / N_HEADS)


def _nt(a, b):
    return lax.dot_general(a, b, (((1,), (1,)), ((), ())), preferred_element_type=F32)


def _split_bf16(x):
    hi = x.astype(BF16)
    lo = (x - hi.astype(F32)).astype(BF16)
    return hi, lo


def _dot3(a, b, dot):
    ah, al = _split_bf16(a)
    bh, bl = _split_bf16(b)
    return dot(ah, bh) + dot(ah, bl) + dot(al, bh)


def _nn(a, b):
    return jnp.dot(a, b, preferred_element_type=F32)


def _rms(x, g):
    ms = jnp.mean(x * x, axis=-1, keepdims=True)
    return x * lax.rsqrt(ms + EPS) * g


def _head_rms(x, g, bd):
    hi, lo = _split_bf16(x * x)
    ssq = _nn(hi, bd) + _nn(lo, bd)
    return x * lax.rsqrt(ssq * (1.0 / HEAD_DIM) + EPS) * g


def _silu(z):
    return z / (1.0 + jnp.exp(-z))


def _top3_mask(gate, blk_iota, limit):
    sel = jnp.zeros(gate.shape, F32)
    g = gate
    for _ in range(MOBA_TOPK):
        m = jnp.max(g, axis=-1, keepdims=True)
        idx = jnp.min(jnp.where(g == m, blk_iota, 1e9), axis=-1, keepdims=True)
        pick = blk_iota == idx
        sel = jnp.where(pick & (idx < limit), 1.0, sel)
        g = jnp.where(pick, NEG_INF, g)
    return sel


def _pool_lane_consts():
    lane = lax.broadcasted_iota(jnp.int32, (1, W_GROUP), 1)
    w = jnp.where(lane < 64, 2, jnp.where(lane < 128, 4, jnp.where(lane < 192, 8, 16)))
    return lane, w


def _memkv_kernel(mem_ref, g_ref, w_ref, gmk_ref, bd_ref, mkt_ref, mvt_ref, mktb_ref, mvb_ref):
    h = _rms(mem_ref[...], g_ref[...]).astype(BF16)
    kv = _nn(h, w_ref[...])
    mkt = _head_rms(kv[:, :W_GROUP], gmk_ref[...], bd_ref[...]).T
    mv = kv[:, W_GROUP:]
    mkt_ref[...] = mkt
    mvt_ref[...] = mv.T
    mktb_ref[...] = mkt.astype(BF16)
    mvb_ref[...] = mv.astype(BF16)


def _memkv(mem_prompt, g_mem, w_mem_kv_bf, g_mk_t, bd):
    depth = g_mem.shape[0]
    batch = mem_prompt.shape[0]
    out = lambda dt: jax.ShapeDtypeStruct((depth, batch, N_MEM, W_GROUP), dt)
    ospec = pl.BlockSpec((None, None, N_MEM, W_GROUP), lambda l, b: (l, b, 0, 0))
    return pl.pallas_call(
        _memkv_kernel,
        grid=(depth, batch),
        in_specs=[
            pl.BlockSpec((None, N_MEM, D_MODEL), lambda l, b: (b, 0, 0)),
            pl.BlockSpec((None, 1, D_MODEL), lambda l, b: (l, 0, 0)),
            pl.BlockSpec((None, D_MODEL, 2 * W_GROUP), lambda l, b: (l, 0, 0)),
            pl.BlockSpec((None, 1, W_GROUP), lambda l, b: (l, 0, 0)),
            pl.BlockSpec((W_GROUP, W_GROUP), lambda l, b: (0, 0)),
        ],
        out_specs=[ospec, ospec, ospec, ospec],
        out_shape=[out(F32), out(F32), out(BF16), out(BF16)],
        compiler_params=pltpu.CompilerParams(dimension_semantics=("arbitrary", "arbitrary")),
        name="memkv",
    )(mem_prompt, g_mem, w_mem_kv_bf, g_mk_t, bd)


def _prompt_front_kernel(x_ref, gn_ref, win_ref, cw_ref, cb_ref, pw_ref, ps_ref, gq_ref, gk_ref, gmq_ref,
                         bd_ref, memkt_ref, memv_ref,
                         q_ref, kt_ref, vt_ref, ktb_ref, vb_ref, kmean_ref,
                         yconv_ref, ypool_ref, ymem_ref, az_ref, cstate_ref, pstate_ref,
                         u_scr, e_scr, s2_scr, s4_scr, s8_scr):
    t = pl.program_id(1)
    T = ROW_TILE

    @pl.when(t == 0)
    def _():
        u_scr[0:CONV_HALO, :] = jnp.zeros((CONV_HALO, W_GROUP), F32)
        e_scr[0:POOL_HALO, :] = jnp.zeros((POOL_HALO, W_GROUP), F32)

    h = _rms(x_ref[...], gn_ref[...]).astype(BF16)
    bd = bd_ref[...]

    def proj(g):
        return _nn(h, win_ref[:, g * W_GROUP:(g + 1) * W_GROUP])

    u = proj(2) * proj(0)
    u_scr[CONV_HALO:CONV_HALO + T, :] = u
    cw = cw_ref[...]
    conv = (cb_ref[...] + cw[0:1] * u_scr[CONV_HALO - 2:CONV_HALO - 2 + T, :]
            + cw[1:2] * u_scr[CONV_HALO - 1:CONV_HALO - 1 + T, :] + cw[2:3] * u)
    yconv_ref[...] = (proj(1) * conv * _silu(proj(3))).astype(BF16)
    cstate_ref[...] = u_scr[CONV_HALO + T - 2:CONV_HALO + T, :]
    u_scr[0:CONV_HALO, :] = u_scr[T:T + CONV_HALO, :]

    H = POOL_HALO
    pv = proj(4)
    e_scr[H:H + T, :] = pv
    s2_scr[8:H + T, :] = e_scr[8:H + T, :] + e_scr[7:H + T - 1, :]
    s4_scr[16:H + T, :] = s2_scr[16:H + T, :] + s2_scr[14:H + T - 2, :]
    s8_scr[24:H + T, :] = s4_scr[24:H + T, :] + s4_scr[20:H + T - 4, :]
    s16 = s8_scr[H:H + T, :] + s8_scr[H - 8:H + T - 8, :]
    lane, wlane = _pool_lane_consts()
    wsum = jnp.where(lane < 64, s2_scr[H:H + T, :],
                     jnp.where(lane < 128, s4_scr[H:H + T, :],
                               jnp.where(lane < 192, s8_scr[H:H + T, :], s16)))
    pos = t * T + lax.broadcasted_iota(jnp.int32, (T, 1), 0)
    cnt = jnp.minimum(pos + 1, wlane).astype(F32)
    pooled = (wsum / cnt - pv).astype(BF16)
    pool_out = _nn(pooled, pw_ref[...]) * ps_ref[...]
    ypool_ref[...] = (pool_out * _silu(proj(5))).astype(BF16)
    pstate_ref[...] = e_scr[H + T - POOL_STATE:H + T, :]
    e_scr[0:H, :] = e_scr[T:T + H, :]

    q_ref[...] = _head_rms(proj(6), gq_ref[...], bd) * (HEAD_DIM ** -0.5)
    k = _head_rms(proj(7), gk_ref[...], bd)
    v = proj(8)
    kt = k.T
    kt_ref[...] = kt
    vt_ref[...] = v.T
    ktb_ref[...] = kt.astype(BF16)
    vb_ref[...] = v.astype(BF16)
    kmean_ref[...] = jnp.mean(k, axis=0, keepdims=True)
    az_ref[...] = _silu(proj(9)).astype(BF16)

    mq = (_head_rms(proj(10), gmq_ref[...], bd) * (HEAD_DIM ** -0.5)).astype(BF16)
    mz = _silu(proj(11))
    for hd in range(N_HEADS):
        hs = slice(hd * HEAD_DIM, (hd + 1) * HEAD_DIM)
        s = _nn(mq[:, hs], memkt_ref[hs, :])
        p = jnp.exp(s - jnp.max(s, axis=-1, keepdims=True))
        l = jnp.sum(p, axis=-1, keepdims=True)
        o = _nn(p.astype(BF16), memv_ref[:, hs]) / l
        ymem_ref[:, hs] = (o * mz[:, hs]).astype(BF16)


def _prompt_front(x, lw, memkt_bf, memv_bf):
    batch, seq, _ = x.shape
    nt = seq // ROW_TILE
    row = lambda b, t: (b, t, 0)
    const2 = lambda b, t: (0, 0)
    tile = lambda dt: jax.ShapeDtypeStruct((batch, seq, W_GROUP), dt)
    tile_t = jax.ShapeDtypeStruct((batch, W_GROUP, seq), F32)
    blocks = lambda dt: jax.ShapeDtypeStruct((batch, nt, ROW_TILE, W_GROUP), dt)
    tile_spec = pl.BlockSpec((None, ROW_TILE, W_GROUP), row)
    tile_t_spec = pl.BlockSpec((None, W_GROUP, ROW_TILE), lambda b, t: (b, 0, t))
    block_spec = pl.BlockSpec((None, None, ROW_TILE, W_GROUP), lambda b, t: (b, t, 0, 0))
    vec = pl.BlockSpec((1, W_GROUP), const2)
    return pl.pallas_call(
        _prompt_front_kernel,
        grid=(batch, nt),
        in_specs=[
            pl.BlockSpec((None, ROW_TILE, D_MODEL), row),
            pl.BlockSpec((1, D_MODEL), const2),
            pl.BlockSpec((D_MODEL, N_IN_SPLITS * W_GROUP), const2),
            pl.BlockSpec((CONV_W, W_GROUP), const2),
            vec,
            pl.BlockSpec((W_GROUP, W_GROUP), const2),
            vec, vec, vec, vec,
            pl.BlockSpec((W_GROUP, W_GROUP), const2),
            pl.BlockSpec((None, W_GROUP, N_MEM), lambda b, t: (b, 0, 0)),
            pl.BlockSpec((None, N_MEM, W_GROUP), lambda b, t: (b, 0, 0)),
        ],
        out_specs=[
            tile_spec, tile_t_spec, tile_t_spec, block_spec, block_spec,
            pl.BlockSpec((None, None, 1, W_GROUP), lambda b, t: (b, t, 0, 0)),
            tile_spec, tile_spec, tile_spec, tile_spec,
            pl.BlockSpec((None, CONV_W - 1, W_GROUP), lambda b, t: (b, 0, 0)),
            pl.BlockSpec((None, POOL_STATE, W_GROUP), lambda b, t: (b, 0, 0)),
        ],
        out_shape=[
            tile(F32), tile_t, tile_t, blocks(BF16), blocks(BF16),
            jax.ShapeDtypeStruct((batch, nt, 1, W_GROUP), F32),
            tile(BF16), tile(BF16), tile(BF16), tile(BF16),
            jax.ShapeDtypeStruct((batch, CONV_W - 1, W_GROUP), F32),
            jax.ShapeDtypeStruct((batch, POOL_STATE, W_GROUP), F32),
        ],
        scratch_shapes=[
            pltpu.VMEM((CONV_HALO + ROW_TILE, W_GROUP), F32),
            pltpu.VMEM((POOL_HALO + ROW_TILE, W_GROUP), F32),
            pltpu.VMEM((POOL_HALO + ROW_TILE, W_GROUP), F32),
            pltpu.VMEM((POOL_HALO + ROW_TILE, W_GROUP), F32),
            pltpu.VMEM((POOL_HALO + ROW_TILE, W_GROUP), F32),
        ],
        compiler_params=pltpu.CompilerParams(dimension_semantics=("arbitrary", "arbitrary"),
                                             vmem_limit_bytes=VMEM_LIMIT),
        name="prompt_front",
    )(x, lw["g_norm"], lw["w_in"], lw["conv_w"], lw["conv_b"], lw["pool_w"], lw["pool_scale"],
      lw["g_q"], lw["g_k"], lw["g_mq"], lw["bd"], memkt_bf, memv_bf)


def _prompt_moba_kernel(q_ref, ktb_ref, vb_ref, kmean_ref, az_ref, out_ref):
    i = pl.program_id(1)
    T = ROW_TILE
    nb = kmean_ref.shape[0]
    q = q_ref[...]
    qb = q.astype(BF16)
    km = kmean_ref[...]
    rel = (lax.broadcasted_iota(jnp.int32, (T, T), 0) - lax.broadcasted_iota(jnp.int32, (T, T), 1)).astype(F32)
    blk_iota = lax.broadcasted_iota(jnp.int32, (T, nb), 1).astype(F32)
    i_f = i.astype(F32)

    for hd in range(N_HEADS):
        hs = slice(hd * HEAD_DIM, (hd + 1) * HEAD_DIM)
        slope = _slope(hd)
        q_h = qb[:, hs]
        gate = jnp.where(blk_iota < i_f, _dot3(q[:, hs], km[:, hs], _nt), NEG_INF)
        sel = _top3_mask(gate, blk_iota, i_f)

        s = _nn(q_h, ktb_ref[i, hs, :]) - slope * rel
        s = jnp.where(rel >= 0.0, s, NEG_INF)
        m0 = jnp.max(s, axis=-1, keepdims=True)
        p = jnp.exp(s - m0)
        l0 = jnp.sum(p, axis=-1, keepdims=True)
        acc0 = _nn(p.astype(BF16), vb_ref[i, :, hs])

        def body(n, carry, q_h=q_h, sel=sel, slope=slope, hs=hs):
            m, l, acc = carry
            s = _nn(q_h, ktb_ref[n, hs, :])
            s = s - slope * (rel + ((i - n) * T).astype(F32))
            picked = jnp.sum(jnp.where(blk_iota == n.astype(F32), sel, 0.0), axis=-1, keepdims=True)
            s = jnp.where(picked > 0.0, s, NEG_INF)
            m_new = jnp.maximum(m, jnp.max(s, axis=-1, keepdims=True))
            alpha = jnp.exp(m - m_new)
            p = jnp.exp(s - m_new)
            l = alpha * l + jnp.sum(p, axis=-1, keepdims=True)
            acc = alpha * acc + _nn(p.astype(BF16), vb_ref[n, :, hs])
            return m_new, l, acc

        _, l, acc = lax.fori_loop(0, i, body, (m0, l0, acc0))
        out_ref[:, hs] = ((acc / l) * az_ref[:, hs].astype(F32)).astype(BF16)


def _prompt_moba(q, ktb, vb, kmean, az):
    batch, seq, _ = q.shape
    nt = seq // ROW_TILE
    row = lambda b, t: (b, t, 0)
    return pl.pallas_call(
        _prompt_moba_kernel,
        grid=(batch, nt),
        in_specs=[
            pl.BlockSpec((None, ROW_TILE, W_GROUP), row),
            pl.BlockSpec((None, nt, W_GROUP, ROW_TILE), lambda b, t: (b, 0, 0, 0)),
            pl.BlockSpec((None, nt, ROW_TILE, W_GROUP), lambda b, t: (b, 0, 0, 0)),
            pl.BlockSpec((None, nt, W_GROUP), lambda b, t: (b, 0, 0)),
            pl.BlockSpec((None, ROW_TILE, W_GROUP), row),
        ],
        out_specs=pl.BlockSpec((None, ROW_TILE, W_GROUP), row),
        out_shape=jax.ShapeDtypeStruct((batch, seq, W_GROUP), BF16),
        compiler_params=pltpu.CompilerParams(dimension_semantics=("arbitrary", "arbitrary"),
                                             vmem_limit_bytes=VMEM_LIMIT),
        name="prompt_moba",
    )(q, ktb, vb, kmean, az)


def _out_kernel(x_ref, yconv_ref, ypool_ref, yatt_ref, ymem_ref, w_ref, y_ref):
    acc = x_ref[...]
    for g, ref in enumerate((yconv_ref, ypool_ref, yatt_ref, ymem_ref)):
        acc = acc + _nn(ref[...].astype(BF16), w_ref[g * W_GROUP:(g + 1) * W_GROUP, :])
    y_ref[...] = acc


def _out_proj(x2, yconv, ypool, yatt, ymem, w_out_bf):
    rows = x2.shape[0]
    row = lambda r: (r, 0)
    part = pl.BlockSpec((ROW_TILE, W_GROUP), row)
    return pl.pallas_call(
        _out_kernel,
        grid=(rows // ROW_TILE,),
        in_specs=[pl.BlockSpec((ROW_TILE, D_MODEL), row), part, part, part, part,
                  pl.BlockSpec((D_MODEL, D_MODEL), lambda r: (0, 0))],
        out_specs=pl.BlockSpec((ROW_TILE, D_MODEL), row),
        out_shape=jax.ShapeDtypeStruct((rows, D_MODEL), F32),
        compiler_params=pltpu.CompilerParams(dimension_semantics=("arbitrary",),
                                             vmem_limit_bytes=VMEM_LIMIT),
        name="out_proj",
    )(x2, yconv, ypool, yatt, ymem, w_out_bf)


def _sample_front_kernel(start, x_ref, gn_ref, win_ref, cw_ref, cb_ref, pw_ref, ps_ref, gq_ref, gk_ref, gmq_ref,
                         bd_ref, sconv_ref, spool_ref,
                         q_ref, k_ref, v_ref, mq_ref, az_ref, mz_ref, yconv_ref, ypool_ref, cstate_ref, pstate_ref,
                         u_scr, e_scr):
    nb, ts = sconv_ref.shape[0], u_scr.shape[1] - CONV_HALO
    rows = nb * ts
    h = _rms(x_ref[...], gn_ref[...]).astype(BF16)
    bd = bd_ref[...]

    def proj(g):
        return _nn(h, win_ref[:, g * W_GROUP:(g + 1) * W_GROUP])

    def to3(a):
        return a.reshape(nb, ts, W_GROUP)

    def to2(a):
        return a.reshape(rows, W_GROUP)

    u = proj(2) * proj(0)
    u_scr[:, 0:CONV_HALO - 2, :] = jnp.zeros((nb, CONV_HALO - 2, W_GROUP), F32)
    u_scr[:, CONV_HALO - 2:CONV_HALO, :] = sconv_ref[...]
    u_scr[:, CONV_HALO:CONV_HALO + ts, :] = to3(u)
    cw = cw_ref[...]
    conv3 = (cw[0:1] * u_scr[:, CONV_HALO - 2:CONV_HALO - 2 + ts, :]
             + cw[1:2] * u_scr[:, CONV_HALO - 1:CONV_HALO - 1 + ts, :]
             + cw[2:3] * u_scr[:, CONV_HALO:CONV_HALO + ts, :])
    conv = cb_ref[...] + to2(conv3)
    yconv_ref[...] = proj(1) * conv * _silu(proj(3))
    cstate_ref[...] = u_scr[:, CONV_HALO + ts - 2:CONV_HALO + ts, :]

    H = 16
    pv = proj(4)
    e_scr[:, 0:1, :] = jnp.zeros((nb, 1, W_GROUP), F32)
    e_scr[:, 1:H, :] = spool_ref[...]
    e_scr[:, H:H + ts, :] = to3(pv)
    lane, wlane = _pool_lane_consts()
    wsum = jnp.zeros((nb, ts, W_GROUP), F32)
    for j in range(max(POOL_WINDOWS)):
        shifted = e_scr[:, H - j:H - j + ts, :]
        wsum = wsum + (shifted if j < min(POOL_WINDOWS) else jnp.where(wlane > j, shifted, 0.0))
    pos = start + lax.broadcasted_iota(jnp.int32, (ts, 1), 0)
    cnt = jnp.minimum(pos + 1, wlane).astype(F32)
    pooled = (to2(wsum / cnt) - pv).astype(BF16)
    pool_out = _nn(pooled, pw_ref[...]) * ps_ref[...]
    ypool_ref[...] = pool_out * _silu(proj(5))
    pstate_ref[...] = e_scr[:, H + ts - POOL_STATE:H + ts, :]

    q_ref[...] = _head_rms(proj(6), gq_ref[...], bd) * (HEAD_DIM ** -0.5)
    k_ref[...] = _head_rms(proj(7), gk_ref[...], bd)
    v_ref[...] = proj(8)
    az_ref[...] = _silu(proj(9))
    mq_ref[...] = _head_rms(proj(10), gmq_ref[...], bd) * (HEAD_DIM ** -0.5)
    mz_ref[...] = _silu(proj(11))


def _sample_front(x2, lw, sconv, spool, start):
    nb = sconv.shape[0]
    rows = x2.shape[0]
    ts = rows // nb
    full = lambda shape: pl.BlockSpec(shape, lambda i, n=len(shape): (0,) * n)
    t2 = jax.ShapeDtypeStruct((rows, W_GROUP), F32)
    return pl.pallas_call(
        functools.partial(_sample_front_kernel, start),
        grid=(1,),
        in_specs=[
            full((rows, D_MODEL)), full((1, D_MODEL)), full((D_MODEL, N_IN_SPLITS * W_GROUP)),
            full((CONV_W, W_GROUP)), full((1, W_GROUP)), full((W_GROUP, W_GROUP)),
            full((1, W_GROUP)), full((1, W_GROUP)), full((1, W_GROUP)), full((1, W_GROUP)),
            full((W_GROUP, W_GROUP)),
            full((nb, CONV_W - 1, W_GROUP)), full((nb, POOL_STATE, W_GROUP)),
        ],
        out_specs=[full((rows, W_GROUP))] * 8 + [full((nb, CONV_W - 1, W_GROUP)), full((nb, POOL_STATE, W_GROUP))],
        out_shape=[t2] * 8 + [jax.ShapeDtypeStruct((nb, CONV_W - 1, W_GROUP), F32),
                              jax.ShapeDtypeStruct((nb, POOL_STATE, W_GROUP), F32)],
        scratch_shapes=[
            pltpu.VMEM((nb, CONV_HALO + ts, W_GROUP), F32),
            pltpu.VMEM((nb, 16 + ts, W_GROUP), F32),
        ],
        compiler_params=pltpu.CompilerParams(dimension_semantics=("arbitrary",),
                                             vmem_limit_bytes=VMEM_LIMIT),
        name="sample_front",
    )(x2, lw["g_norm"], lw["w_in"], lw["conv_w"], lw["conv_b"], lw["pool_w"], lw["pool_scale"],
      lw["g_q"], lw["g_k"], lw["g_mq"], lw["bd"], sconv, spool)


def _head_rows(a, ts):
    r = lax.broadcasted_iota(jnp.int32, (N_HEADS * ts, W_GROUP), 0) // ts
    c = lax.broadcasted_iota(jnp.int32, (N_HEADS * ts, W_GROUP), 1) // HEAD_DIM
    return jnp.where(r == c, jnp.concatenate([a] * N_HEADS, axis=0), 0.0)


def _head_diag(o, ts):
    r = lax.broadcasted_iota(jnp.int32, (N_HEADS * ts, W_GROUP), 0) // ts
    c = lax.broadcasted_iota(jnp.int32, (N_HEADS * ts, W_GROUP), 1) // HEAD_DIM
    o = jnp.where(r == c, o, 0.0)
    out = o[0:ts]
    for hd in range(1, N_HEADS):
        out = out + o[hd * ts:(hd + 1) * ts]
    return out


def _sample_moba_kernel(past_len, pt_ref, q_ref, knew_ref, vnew_ref, mq_ref, az_ref, mz_ref, memkt_ref, memvt_ref,
                        *refs):
    P = PAGES_PER_STEP
    BPS = P // PAGES_PER_BLOCK
    kpages, vpages = refs[:P], refs[P:2 * P]
    yatt_ref, ymem_ref = refs[2 * P], refs[2 * P + 1]
    s_scr, p_scr, ksum_scr, acc_scr, l_scr, pown_scr = refs[2 * P + 2:]
    ph, c = pl.program_id(1), pl.program_id(2)
    nc = pl.num_programs(2)
    ts = q_ref.shape[0]
    R = N_HEADS * ts
    n_pages = past_len // PAGE_SIZE
    nblk = past_len // MOBA_BLOCK
    qrows = _head_rows(q_ref[...], ts)
    row_t = (lax.broadcasted_iota(jnp.int32, (R, 1), 0) % ts).astype(F32)
    row_h = lax.broadcasted_iota(jnp.int32, (R, 1), 0) // ts
    slope = jnp.where(row_h == 0, _slope(0), jnp.where(row_h == 1, _slope(1),
                                                        jnp.where(row_h == 2, _slope(2), _slope(3))))
    qpos = past_len + row_t

    @pl.when(ph == 0)
    def _():
        qb = qrows.astype(BF16)
        lane8 = lax.broadcasted_iota(jnp.int32, (W_GROUP, BPS), 1)
        sums = jnp.zeros((W_GROUP, BPS), F32)
        for j in range(0, P, PAGES_PER_BLOCK):
            ka, kb = kpages[j][...], kpages[j + 1][...]
            s_scr[c * P + j] = _nn(qb, ka.astype(BF16))
            s_scr[c * P + j + 1] = _nn(qb, kb.astype(BF16))
            tot = jnp.sum(ka + kb, axis=1, keepdims=True)
            sums = jnp.where(lane8 == j // PAGES_PER_BLOCK, tot, sums)
        ksum_scr[c] = sums

    @pl.when((ph == 0) & (c == nc - 1))
    def _():
        gate = jnp.concatenate(
            [_dot3(qrows, ksum_scr[cc] * (1.0 / MOBA_BLOCK), _nn) for cc in range(n_pages // P)], axis=1)
        blk_iota = lax.broadcasted_iota(jnp.int32, (R, nblk), 1).astype(F32)
        sel = _top3_mask(gate, blk_iota, float(nblk))
        own = []
        m = jnp.full((R, 1), NEG_INF, F32)
        for j in range(ts):
            sj = jnp.sum(qrows * knew_ref[j:j + 1, :], axis=-1, keepdims=True) - slope * (row_t - float(j))
            sj = jnp.where(row_t >= float(j), sj, NEG_INF)
            own.append(sj)
            m = jnp.maximum(m, sj)
        col = lax.broadcasted_iota(jnp.int32, (R, PAGE_SIZE), 1).astype(F32)

        def mask_body(pg, m):
            blk = (pg // PAGES_PER_BLOCK).astype(F32)
            s = s_scr[pg] - slope * (qpos - (pg.astype(F32) * PAGE_SIZE + col))
            picked = jnp.sum(jnp.where(blk_iota == blk, sel, 0.0), axis=-1, keepdims=True)
            s = jnp.where(picked > 0.0, s, NEG_INF)
            s_scr[pg] = s
            return jnp.maximum(m, jnp.max(s, axis=-1, keepdims=True))

        m = lax.fori_loop(0, n_pages, mask_body, m)
        l = jnp.zeros((R, 1), F32)
        for j in range(ts):
            pj = jnp.exp(own[j] - m)
            pown_scr[:, j:j + 1] = pj
            l = l + pj

        def exp_body(pg, l):
            p = jnp.exp(s_scr[pg] - m)
            p_scr[pg] = p.astype(BF16)
            return l + jnp.sum(p, axis=-1, keepdims=True)

        l_scr[...] = lax.fori_loop(0, n_pages, exp_body, l)

    @pl.when((ph == 1) & (c == 0))
    def _():
        acc_scr[...] = jnp.zeros(acc_scr.shape, F32)

    @pl.when(ph == 1)
    def _():
        acc = acc_scr[...]
        for j in range(P):
            acc = acc + _nt(p_scr[c * P + j], vpages[j][...].astype(BF16))
        acc_scr[...] = acc

    @pl.when((ph == 1) & (c == nc - 1))
    def _():
        acc = acc_scr[...]
        for j in range(ts):
            acc = acc + pown_scr[:, j:j + 1] * vnew_ref[j:j + 1, :]
        yatt_ref[...] = _head_diag(acc / l_scr[...], ts) * az_ref[...]
        mrows = _head_rows(mq_ref[...], ts).astype(BF16)
        s = _nn(mrows, memkt_ref[...].astype(BF16))
        p = jnp.exp(s - jnp.max(s, axis=-1, keepdims=True))
        lm = jnp.sum(p, axis=-1, keepdims=True)
        o = _nt(p.astype(BF16), memvt_ref[...].astype(BF16)) / lm
        ymem_ref[...] = _head_diag(o, ts) * mz_ref[...]


def _sample_moba(layer, page_table, q, knew, vnew, mq, az, mz, memkt, memvt, cache_kt, cache_vt):
    nb, n_pages = page_table.shape
    rows = q.shape[0]
    ts = rows // nb
    P = PAGES_PER_STEP
    nc = n_pages // P
    past_len = n_pages * PAGE_SIZE
    R = N_HEADS * ts
    row = lambda b, ph, c, pt: (b, 0)
    seq = pl.BlockSpec((ts, W_GROUP), row)
    mem = pl.BlockSpec((None, None, W_GROUP, N_MEM), lambda b, ph, c, pt: (layer, b, 0, 0))

    def kpage(j):
        return pl.BlockSpec((None, None, W_GROUP, PAGE_SIZE),
                            lambda b, ph, c, pt: (layer, pt[b, jnp.where(ph == 0, c, nc - 1) * P + j], 0, 0))

    def vpage(j):
        return pl.BlockSpec((None, None, W_GROUP, PAGE_SIZE),
                            lambda b, ph, c, pt: (layer, pt[b, jnp.where(ph == 0, 0, c) * P + j], 0, 0))

    out = jax.ShapeDtypeStruct((rows, W_GROUP), F32)
    grid_spec = pltpu.PrefetchScalarGridSpec(
        num_scalar_prefetch=1,
        grid=(nb, 2, nc),
        in_specs=[seq, seq, seq, seq, seq, seq, mem, mem] + [kpage(j) for j in range(P)] + [vpage(j) for j in range(P)],
        out_specs=[seq, seq],
        scratch_shapes=[
            pltpu.VMEM((n_pages, R, PAGE_SIZE), F32),
            pltpu.VMEM((n_pages, R, PAGE_SIZE), BF16),
            pltpu.VMEM((nc, W_GROUP, P // PAGES_PER_BLOCK), F32),
            pltpu.VMEM((R, W_GROUP), F32),
            pltpu.VMEM((R, 1), F32),
            pltpu.VMEM((R, ts), F32),
        ],
    )
    return pl.pallas_call(
        functools.partial(_sample_moba_kernel, past_len),
        grid_spec=grid_spec,
        out_shape=[out, out],
        compiler_params=pltpu.CompilerParams(dimension_semantics=("arbitrary", "arbitrary", "arbitrary"),
                                             vmem_limit_bytes=VMEM_LIMIT),
        name="sample_moba",
    )(page_table, q, knew, vnew, mq, az, mz, memkt, memvt, *([cache_kt] * P), *([cache_vt] * P))


def _same_head_matrix():
    r = lax.broadcasted_iota(jnp.int32, (W_GROUP, W_GROUP), 0) // HEAD_DIM
    c = lax.broadcasted_iota(jnp.int32, (W_GROUP, W_GROUP), 1) // HEAD_DIM
    return (r == c).astype(BF16)


def _pool_block_diag(pool_w_l):
    out = jnp.zeros((W_GROUP, W_GROUP), pool_w_l.dtype)
    for g in range(len(POOL_WINDOWS)):
        out = out.at[g * 64:(g + 1) * 64, g * 64:(g + 1) * 64].set(pool_w_l[g])
    return out


def _tokens_minor(a):
    lead = a.shape[:-3]
    n = len(lead)
    perm = tuple(range(n)) + (n + 1, n + 2, n)
    return a.transpose(perm).reshape(lead + (W_GROUP, a.shape[-3]))


def _tokens_major(a_t):
    lead = a_t.shape[:-2]
    n = len(lead)
    perm = tuple(range(n)) + (n + 2, n, n + 1)
    return a_t.reshape(lead + (N_HEADS, HEAD_DIM, a_t.shape[-1])).transpose(perm)


def kernel(x_prompt, x_sample, cache_k, cache_v, page_table, state_conv, state_pool, cache_mem_k, cache_mem_v,
           mem_prompt, g_norm, w_in, w_out, conv_w, conv_b, pool_w, pool_scale, g_q, g_k, g_mq, g_mk, g_mem,
           w_mem_kv):
    depth = w_in.shape[0]
    bp, seq, _ = x_prompt.shape
    bs, ts, _ = x_sample.shape
    n_pages = page_table.shape[1]
    past_len = n_pages * PAGE_SIZE
    bd = _same_head_matrix()
    tile4 = lambda g: jnp.tile(g, (1, N_HEADS))[:, None, :]

    w_in_bf = w_in.astype(BF16)
    w_out_bf = w_out.astype(BF16)
    g_q_t, g_k_t, g_mq_t, g_mk_t = tile4(g_q), tile4(g_k), tile4(g_mq), tile4(g_mk)
    layer_w = []
    for l in range(depth):
        layer_w.append(dict(
            g_norm=g_norm[l][None, :], w_in=w_in_bf[l], conv_w=conv_w[l], conv_b=conv_b[l][None, :],
            pool_w=_pool_block_diag(pool_w[l]).astype(BF16), pool_scale=pool_scale[l][None, :],
            g_q=g_q_t[l], g_k=g_k_t[l], g_mq=g_mq_t[l], bd=bd))

    mkt_all, mvt_all, mkt_bf, mv_bf = _memkv(mem_prompt, g_mem[:, None, :], w_mem_kv.astype(BF16), g_mk_t, bd)

    cache_kt = _tokens_minor(cache_k)
    cache_vt = _tokens_minor(cache_v)
    memkt_s = _tokens_minor(cache_mem_k)
    memvt_s = _tokens_minor(cache_mem_v)

    xp = x_prompt
    xs = x_sample.reshape(bs * ts, D_MODEL)
    kp_l, vp_l, ks_l, vs_l, cp_l, cs_l, pp_l, ps_l = ([] for _ in range(8))
    for l in range(depth):
        lw = layer_w[l]
        (q, kt, vt, ktb, vb, kmean, yconv, ypool, ymem, az, cst, pst) = _prompt_front(xp, lw, mkt_bf[l], mv_bf[l])
        yatt = _prompt_moba(q, ktb, vb, kmean.reshape(bp, seq // ROW_TILE, W_GROUP), az)
        flat = lambda a: a.reshape(bp * seq, W_GROUP)
        xp = _out_proj(xp.reshape(bp * seq, D_MODEL), flat(yconv), flat(ypool), flat(yatt), flat(ymem),
                       w_out_bf[l]).reshape(bp, seq, D_MODEL)
        kp_l.append(_tokens_major(kt))
        vp_l.append(_tokens_major(vt))
        cp_l.append(cst)
        pp_l.append(pst)
        (qs, ksn, vsn, mqs, azs, mzs, yconv_s, ypool_s, cst_s, pst_s) = _sample_front(
            xs, lw, state_conv[l], state_pool[l], past_len)
        yatt_s, ymem_s = _sample_moba(l, page_table, qs, ksn, vsn, mqs, azs, mzs, memkt_s, memvt_s,
                                      cache_kt, cache_vt)
        xs = _out_proj(xs, yconv_s, ypool_s, yatt_s, ymem_s, w_out_bf[l])
        ks_l.append(ksn.reshape(bs, ts, N_HEADS, HEAD_DIM))
        vs_l.append(vsn.reshape(bs, ts, N_HEADS, HEAD_DIM))
        cs_l.append(cst_s)
        ps_l.append(pst_s)

    return (xp, xs.reshape(bs, ts, D_MODEL), jnp.stack(kp_l), jnp.stack(vp_l), jnp.stack(ks_l), jnp.stack(vs_l),
            jnp.stack(cp_l), jnp.stack(cs_l), jnp.stack(pp_l), jnp.stack(ps_l),
            _tokens_major(mkt_all), _tokens_major(mvt_all))
```

```python
import functools

import jax
import jax.numpy as jnp
from jax import lax
from jax.experimental import pallas as pl
from jax.experimental.pallas import tpu as pltpu

F32 = jnp.float32
BF16 = jnp.bfloat16

D_MODEL = 1024
W_GROUP = 256
N_HEADS = 4
HEAD_DIM = 64
N_MEM = 256
CONV_W = 3
POOL_WINDOWS = (2, 4, 8, 16)
POOL_STATE = 15
MOBA_BLOCK = 256
MOBA_TOPK = 3
PAGE_SIZE = 128
N_IN_SPLITS = 12
EPS = 1e-6
NEG_INF = float("-inf")

ROW_TILE = 256
POOL_HALO = 32
CONV_HALO = 8
PAGES_PER_STEP = 16
PAGES_PER_BLOCK = MOBA_BLOCK // PAGE_SIZE
VMEM_LIMIT = 56 * 1024 * 1024

MAX_BLOCKS = 32
AUG_LANES = 64
K_GROUP_LANES = HEAD_DIM + AUG_LANES
K_AUG_LANES = N_HEADS * K_GROUP_LANES
V_AUG_ROWS = 80
MASKED = -1e30


def _slope(h):
    return 2.0 ** (-8.0 * (h + 1) / N_HEADS)


def _nt(a, b):
    return lax.dot_general(a, b, (((1,), (1,)), ((), ())), preferred_element_type=F32)


def _split_bf16(x):
    hi = x.astype(BF16)
    lo = (x - hi.astype(F32)).astype(BF16)
    return hi, lo


def _dot3(a, b, dot):
    ah, al = _split_bf16(a)
    bh, bl = _split_bf16(b)
    return dot(ah, bh) + dot(ah, bl) + dot(al, bh)


def _nn(a, b):
    return jnp.dot(a, b, preferred_element_type=F32)


def _rms(x, g):
    ms = jnp.mean(x * x, axis=-1, keepdims=True)
    return x * lax.rsqrt(ms + EPS) * g


def _head_rms(x, g, bd):
    hi, lo = _split_bf16(x * x)
    ssq = _nn(hi, bd) + _nn(lo, bd)
    return x * lax.rsqrt(ssq * (1.0 / HEAD_DIM) + EPS) * g


def _silu(z):
    return z / (1.0 + jnp.exp(-z))


def _top3_mask(gate, blk_iota, limit, axis):
    sel = jnp.zeros(gate.shape, F32)
    g = gate
    for _ in range(MOBA_TOPK):
        m = jnp.max(g, axis=axis, keepdims=True)
        idx = jnp.min(jnp.where(g == m, blk_iota, 1e9), axis=axis, keepdims=True)
        pick = blk_iota == idx
        sel = jnp.where(pick & (idx < limit), 1.0, sel)
        g = jnp.where(pick, NEG_INF, g)
    return sel


def _pool_lane_consts():
    lane = lax.broadcasted_iota(jnp.int32, (1, W_GROUP), 1)
    w = jnp.where(lane < 64, 2, jnp.where(lane < 128, 4, jnp.where(lane < 192, 8, 16)))
    return lane, w


def _memkv_kernel(mem_ref, g_ref, w_ref, gmk_ref, bd_ref, mkt_ref, mvt_ref, mktb_ref, mvb_ref):
    h = _rms(mem_ref[...], g_ref[...]).astype(BF16)
    kv = _nn(h, w_ref[...])
    mkt = _head_rms(kv[:, :W_GROUP], gmk_ref[...], bd_ref[...]).T
    mv = kv[:, W_GROUP:]
    mkt_ref[...] = mkt
    mvt_ref[...] = mv.T
    mktb_ref[...] = mkt.astype(BF16)
    mvb_ref[...] = mv.astype(BF16)


def _memkv(mem_prompt, g_mem, w_mem_kv_bf, g_mk_t, bd):
    depth = g_mem.shape[0]
    batch = mem_prompt.shape[0]
    out = lambda dt: jax.ShapeDtypeStruct((depth, batch, N_MEM, W_GROUP), dt)
    ospec = pl.BlockSpec((None, None, N_MEM, W_GROUP), lambda l, b: (l, b, 0, 0))
    return pl.pallas_call(
        _memkv_kernel,
        grid=(depth, batch),
        in_specs=[
            pl.BlockSpec((None, N_MEM, D_MODEL), lambda l, b: (b, 0, 0)),
            pl.BlockSpec((None, 1, D_MODEL), lambda l, b: (l, 0, 0)),
            pl.BlockSpec((None, D_MODEL, 2 * W_GROUP), lambda l, b: (l, 0, 0)),
            pl.BlockSpec((None, 1, W_GROUP), lambda l, b: (l, 0, 0)),
            pl.BlockSpec((W_GROUP, W_GROUP), lambda l, b: (0, 0)),
        ],
        out_specs=[ospec, ospec, ospec, ospec],
        out_shape=[out(F32), out(F32), out(BF16), out(BF16)],
        compiler_params=pltpu.CompilerParams(dimension_semantics=("arbitrary", "arbitrary")),
        name="memkv",
    )(mem_prompt, g_mem, w_mem_kv_bf, g_mk_t, bd)


def _prompt_front_kernel(x_ref, gn_ref, win_ref, cw_ref, cb_ref, pw_ref, ps_ref, gq_ref, gk_ref, gmq_ref,
                         bd_ref, memkt_ref, memv_ref,
                         q_ref, kt_ref, vt_ref, kaug_ref, vaug_ref, kmean_ref,
                         yconv_ref, ypool_ref, ymem_ref, az_ref, cstate_ref, pstate_ref,
                         u_scr, e_scr, s2_scr, s4_scr, s8_scr):
    t = pl.program_id(1)
    T = ROW_TILE

    @pl.when(t == 0)
    def _():
        u_scr[0:CONV_HALO, :] = jnp.zeros((CONV_HALO, W_GROUP), F32)
        e_scr[0:POOL_HALO, :] = jnp.zeros((POOL_HALO, W_GROUP), F32)

    h = _rms(x_ref[...], gn_ref[...]).astype(BF16)
    bd = bd_ref[...]

    def proj(g):
        return _nn(h, win_ref[:, g * W_GROUP:(g + 1) * W_GROUP])

    u = proj(2) * proj(0)
    u_scr[CONV_HALO:CONV_HALO + T, :] = u
    cw = cw_ref[...]
    conv = (cb_ref[...] + cw[0:1] * u_scr[CONV_HALO - 2:CONV_HALO - 2 + T, :]
            + cw[1:2] * u_scr[CONV_HALO - 1:CONV_HALO - 1 + T, :] + cw[2:3] * u)
    yconv_ref[...] = (proj(1) * conv * _silu(proj(3))).astype(BF16)
    cstate_ref[...] = u_scr[CONV_HALO + T - 2:CONV_HALO + T, :]
    u_scr[0:CONV_HALO, :] = u_scr[T:T + CONV_HALO, :]

    H = POOL_HALO
    pv = proj(4)
    e_scr[H:H + T, :] = pv
    s2_scr[8:H + T, :] = e_scr[8:H + T, :] + e_scr[7:H + T - 1, :]
    s4_scr[16:H + T, :] = s2_scr[16:H + T, :] + s2_scr[14:H + T - 2, :]
    s8_scr[24:H + T, :] = s4_scr[24:H + T, :] + s4_scr[20:H + T - 4, :]
    s16 = s8_scr[H:H + T, :] + s8_scr[H - 8:H + T - 8, :]
    lane, wlane = _pool_lane_consts()
    wsum = jnp.where(lane < 64, s2_scr[H:H + T, :],
                     jnp.where(lane < 128, s4_scr[H:H + T, :],
                               jnp.where(lane < 192, s8_scr[H:H + T, :], s16)))
    pos = t * T + lax.broadcasted_iota(jnp.int32, (T, 1), 0)
    cnt = jnp.minimum(pos + 1, wlane).astype(F32)
    pooled = (wsum / cnt - pv).astype(BF16)
    pool_out = _nn(pooled, pw_ref[...]) * ps_ref[...]
    ypool_ref[...] = (pool_out * _silu(proj(5))).astype(BF16)
    pstate_ref[...] = e_scr[H + T - POOL_STATE:H + T, :]
    e_scr[0:H, :] = e_scr[T:T + H, :]

    q_ref[...] = _head_rms(proj(6), gq_ref[...], bd) * (HEAD_DIM ** -0.5)
    k = _head_rms(proj(7), gk_ref[...], bd)
    v = proj(8)
    vt = v.T
    kt_ref[...] = k.T
    vt_ref[...] = vt
    kmean_ref[...] = jnp.mean(k, axis=0, keepdims=True)
    lane64 = lax.broadcasted_iota(jnp.int32, (T, AUG_LANES), 1)
    key_idx = lax.broadcasted_iota(jnp.int32, (T, AUG_LANES), 0).astype(F32)
    onehot = (lane64 == t).astype(F32)
    ones_row = (lax.broadcasted_iota(jnp.int32, (V_AUG_ROWS - HEAD_DIM, T), 0) == 0).astype(F32)
    kparts, vparts = [], []
    for hd in range(N_HEADS):
        hs = slice(hd * HEAD_DIM, (hd + 1) * HEAD_DIM)
        kparts += [k[:, hs], jnp.where(lane64 == MAX_BLOCKS, _slope(hd) * key_idx, onehot)]
        vparts += [vt[hs, :], ones_row]
    kaug_ref[...] = jnp.concatenate(kparts, axis=1).astype(BF16)
    vaug_ref[...] = jnp.concatenate(vparts, axis=0).astype(BF16)
    az_ref[...] = _silu(proj(9)).astype(BF16)

    mq = (_head_rms(proj(10), gmq_ref[...], bd) * (HEAD_DIM ** -0.5)).astype(BF16)
    mz = _silu(proj(11))
    for hd in range(N_HEADS):
        hs = slice(hd * HEAD_DIM, (hd + 1) * HEAD_DIM)
        s = _nn(mq[:, hs], memkt_ref[hs, :])
        p = jnp.exp(s - jnp.max(s, axis=-1, keepdims=True))
        l = jnp.sum(p, axis=-1, keepdims=True)
        o = _nn(p.astype(BF16), memv_ref[:, hs]) / l
        ymem_ref[:, hs] = (o * mz[:, hs]).astype(BF16)


def _prompt_front(x, lw, memkt_bf, memv_bf):
    batch, seq, _ = x.shape
    nt = seq // ROW_TILE
    row = lambda b, t: (b, t, 0)
    const2 = lambda b, t: (0, 0)
    tile = lambda dt: jax.ShapeDtypeStruct((batch, seq, W_GROUP), dt)
    tile_t = jax.ShapeDtypeStruct((batch, W_GROUP, seq), F32)
    assert nt <= MAX_BLOCKS
    tile_spec = pl.BlockSpec((None, ROW_TILE, W_GROUP), row)
    tile_t_spec = pl.BlockSpec((None, W_GROUP, ROW_TILE), lambda b, t: (b, 0, t))
    kaug_spec = pl.BlockSpec((None, None, ROW_TILE, K_AUG_LANES), lambda b, t: (b, t, 0, 0))
    vaug_spec = pl.BlockSpec((None, None, N_HEADS * V_AUG_ROWS, ROW_TILE), lambda b, t: (b, t, 0, 0))
    vec = pl.BlockSpec((1, W_GROUP), const2)
    return pl.pallas_call(
        _prompt_front_kernel,
        grid=(batch, nt),
        in_specs=[
            pl.BlockSpec((None, ROW_TILE, D_MODEL), row),
            pl.BlockSpec((1, D_MODEL), const2),
            pl.BlockSpec((D_MODEL, N_IN_SPLITS * W_GROUP), const2),
            pl.BlockSpec((CONV_W, W_GROUP), const2),
            vec,
            pl.BlockSpec((W_GROUP, W_GROUP), const2),
            vec, vec, vec, vec,
            pl.BlockSpec((W_GROUP, W_GROUP), const2),
            pl.BlockSpec((None, W_GROUP, N_MEM), lambda b, t: (b, 0, 0)),
            pl.BlockSpec((None, N_MEM, W_GROUP), lambda b, t: (b, 0, 0)),
        ],
        out_specs=[
            tile_spec, tile_t_spec, tile_t_spec, kaug_spec, vaug_spec,
            pl.BlockSpec((None, None, 1, W_GROUP), lambda b, t: (b, t, 0, 0)),
            tile_spec, tile_spec, tile_spec, tile_spec,
            pl.BlockSpec((None, CONV_W - 1, W_GROUP), lambda b, t: (b, 0, 0)),
            pl.BlockSpec((None, POOL_STATE, W_GROUP), lambda b, t: (b, 0, 0)),
        ],
        out_shape=[
            tile(F32), tile_t, tile_t,
            jax.ShapeDtypeStruct((batch, nt, ROW_TILE, K_AUG_LANES), BF16),
            jax.ShapeDtypeStruct((batch, nt, N_HEADS * V_AUG_ROWS, ROW_TILE), BF16),
            jax.ShapeDtypeStruct((batch, nt, 1, W_GROUP), F32),
            tile(BF16), tile(BF16), tile(BF16), tile(BF16),
            jax.ShapeDtypeStruct((batch, CONV_W - 1, W_GROUP), F32),
            jax.ShapeDtypeStruct((batch, POOL_STATE, W_GROUP), F32),
        ],
        scratch_shapes=[
            pltpu.VMEM((CONV_HALO + ROW_TILE, W_GROUP), F32),
            pltpu.VMEM((POOL_HALO + ROW_TILE, W_GROUP), F32),
            pltpu.VMEM((POOL_HALO + ROW_TILE, W_GROUP), F32),
            pltpu.VMEM((POOL_HALO + ROW_TILE, W_GROUP), F32),
            pltpu.VMEM((POOL_HALO + ROW_TILE, W_GROUP), F32),
        ],
        compiler_params=pltpu.CompilerParams(dimension_semantics=("arbitrary", "arbitrary"),
                                             vmem_limit_bytes=VMEM_LIMIT),
        name="prompt_front",
    )(x, lw["g_norm"], lw["w_in"], lw["conv_w"], lw["conv_b"], lw["pool_w"], lw["pool_scale"],
      lw["g_q"], lw["g_k"], lw["g_mq"], lw["bd"], memkt_bf, memv_bf)


def _prompt_moba_kernel(q_ref, kaug_ref, vaug_ref, kmean_ref, az_ref, out_ref, qaug_scr, acc_scr, outt_scr, s_scr):
    i = pl.program_id(1)
    T = ROW_TILE
    nb = kmean_ref.shape[0]
    qt = q_ref[...].T
    km = kmean_ref[...]
    blk_iota = lax.broadcasted_iota(jnp.int32, (nb, T), 0)
    blk_f = blk_iota.astype(F32)
    i_f = i.astype(F32)
    ones_rows = (lax.broadcasted_iota(jnp.int32, (8, T), 0) == 0).astype(F32)
    pad_rows = jnp.zeros((K_GROUP_LANES - HEAD_DIM - MAX_BLOCKS - 8, T), F32)
    causal = lax.broadcasted_iota(jnp.int32, (T, T), 0) <= lax.broadcasted_iota(jnp.int32, (T, T), 1)

    def kgroup(n, hd):
        return kaug_ref[n, :, hd * K_GROUP_LANES:(hd + 1) * K_GROUP_LANES]

    def vgroup(n, hd):
        return vaug_ref[n, hd * V_AUG_ROWS:(hd + 1) * V_AUG_ROWS, :]

    m0, mx0 = [], []
    for hd in range(N_HEADS):
        hs = slice(hd * HEAD_DIM, (hd + 1) * HEAD_DIM)
        qt_h = qt[hs, :]
        gate = jnp.where(blk_f < i_f, _dot3(km[:, hs], qt_h, _nn), NEG_INF)
        sel = _top3_mask(gate, blk_f, i_f, axis=0)
        bias = jnp.where((sel > 0.0) | (blk_iota == i), 0.0, MASKED)
        if nb < MAX_BLOCKS:
            bias = jnp.concatenate([bias, jnp.zeros((MAX_BLOCKS - nb, T), F32)], axis=0)
        qaug_scr[hd] = jnp.concatenate([qt_h, bias, ones_rows, pad_rows], axis=0).astype(BF16)
        s = jnp.where(causal, _nn(kgroup(i, hd), qaug_scr[hd]), NEG_INF)
        m = jnp.max(s, axis=0, keepdims=True)
        acc_scr[hd] = _nn(vgroup(i, hd), jnp.exp(s - m).astype(BF16))
        m0.append(m)
        s = _nn(kgroup(0, hd), qaug_scr[hd])
        s_scr[0, hd] = s
        mx0.append(jnp.max(s, axis=0, keepdims=True))

    def stage(n, cur, nxt, ms, mxs):
        n_next = jnp.minimum(n + 1, i)
        ms_out, mxs_out = [], []
        for hd in range(N_HEADS):
            s_next = _nn(kgroup(n_next, hd), qaug_scr[hd])
            s_scr[nxt, hd] = s_next
            mxs_out.append(jnp.max(s_next, axis=0, keepdims=True))
            cn = jnp.where(n < i, (n - i).astype(F32) * (_slope(hd) * MOBA_BLOCK), NEG_INF)
            m_new = jnp.maximum(ms[hd], mxs[hd] + cn)
            alpha = jnp.exp(ms[hd] - m_new)
            p = jnp.exp(s_scr[cur, hd] - (m_new - cn))
            acc_scr[hd] = alpha * acc_scr[hd] + _nn(vgroup(jnp.minimum(n, i), hd), p.astype(BF16))
            ms_out.append(m_new)
        return tuple(ms_out), tuple(mxs_out)

    def body(j, carry):
        ms, mxs = stage(2 * j, 0, 1, *carry)
        return stage(2 * j + 1, 1, 0, ms, mxs)

    lax.fori_loop(0, (i + 1) // 2, body, (tuple(m0), tuple(mx0)))
    for hd in range(N_HEADS):
        acc = acc_scr[hd]
        outt_scr[hd * HEAD_DIM:(hd + 1) * HEAD_DIM, :] = acc[0:HEAD_DIM] / acc[HEAD_DIM:HEAD_DIM + 1]
    out_ref[...] = (outt_scr[...].T * az_ref[...].astype(F32)).astype(BF16)


def _prompt_moba(q, kaug, vaug, kmean, az):
    batch, seq, _ = q.shape
    nt = seq // ROW_TILE
    row = lambda b, t: (b, t, 0)
    return pl.pallas_call(
        _prompt_moba_kernel,
        grid=(batch, nt),
        in_specs=[
            pl.BlockSpec((None, ROW_TILE, W_GROUP), row),
            pl.BlockSpec((None, nt, ROW_TILE, K_AUG_LANES), lambda b, t: (b, 0, 0, 0)),
            pl.BlockSpec((None, nt, N_HEADS * V_AUG_ROWS, ROW_TILE), lambda b, t: (b, 0, 0, 0)),
            pl.BlockSpec((None, nt, W_GROUP), lambda b, t: (b, 0, 0)),
            pl.BlockSpec((None, ROW_TILE, W_GROUP), row),
        ],
        out_specs=pl.BlockSpec((None, ROW_TILE, W_GROUP), row),
        out_shape=jax.ShapeDtypeStruct((batch, seq, W_GROUP), BF16),
        scratch_shapes=[
            pltpu.VMEM((N_HEADS, K_GROUP_LANES, ROW_TILE), BF16),
            pltpu.VMEM((N_HEADS, V_AUG_ROWS, ROW_TILE), F32),
            pltpu.VMEM((W_GROUP, ROW_TILE), F32),
            pltpu.VMEM((2, N_HEADS, MOBA_BLOCK, ROW_TILE), F32),
        ],
        compiler_params=pltpu.CompilerParams(dimension_semantics=("arbitrary", "arbitrary"),
                                             vmem_limit_bytes=VMEM_LIMIT),
        name="prompt_moba",
    )(q, kaug, vaug, kmean, az)


def _out_kernel(x_ref, yconv_ref, ypool_ref, yatt_ref, ymem_ref, w_ref, y_ref):
    acc = x_ref[...]
    for g, ref in enumerate((yconv_ref, ypool_ref, yatt_ref, ymem_ref)):
        acc = acc + _nn(ref[...].astype(BF16), w_ref[g * W_GROUP:(g + 1) * W_GROUP, :])
    y_ref[...] = acc


def _out_proj(x2, yconv, ypool, yatt, ymem, w_out_bf):
    rows = x2.shape[0]
    row = lambda r: (r, 0)
    part = pl.BlockSpec((ROW_TILE, W_GROUP), row)
    return pl.pallas_call(
        _out_kernel,
        grid=(rows // ROW_TILE,),
        in_specs=[pl.BlockSpec((ROW_TILE, D_MODEL), row), part, part, part, part,
                  pl.BlockSpec((D_MODEL, D_MODEL), lambda r: (0, 0))],
        out_specs=pl.BlockSpec((ROW_TILE, D_MODEL), row),
        out_shape=jax.ShapeDtypeStruct((rows, D_MODEL), F32),
        compiler_params=pltpu.CompilerParams(dimension_semantics=("arbitrary",),
                                             vmem_limit_bytes=VMEM_LIMIT),
        name="out_proj",
    )(x2, yconv, ypool, yatt, ymem, w_out_bf)


def _sample_front_kernel(start, x_ref, gn_ref, win_ref, cw_ref, cb_ref, pw_ref, ps_ref, gq_ref, gk_ref, gmq_ref,
                         bd_ref, sconv_ref, spool_ref,
                         q_ref, k_ref, v_ref, mq_ref, az_ref, mz_ref, yconv_ref, ypool_ref, cstate_ref, pstate_ref,
                         u_scr, e_scr):
    nb, ts = sconv_ref.shape[0], u_scr.shape[1] - CONV_HALO
    rows = nb * ts
    h = _rms(x_ref[...], gn_ref[...]).astype(BF16)
    bd = bd_ref[...]

    def proj(g):
        return _nn(h, win_ref[:, g * W_GROUP:(g + 1) * W_GROUP])

    def to3(a):
        return a.reshape(nb, ts, W_GROUP)

    def to2(a):
        return a.reshape(rows, W_GROUP)

    u = proj(2) * proj(0)
    u_scr[:, 0:CONV_HALO - 2, :] = jnp.zeros((nb, CONV_HALO - 2, W_GROUP), F32)
    u_scr[:, CONV_HALO - 2:CONV_HALO, :] = sconv_ref[...]
    u_scr[:, CONV_HALO:CONV_HALO + ts, :] = to3(u)
    cw = cw_ref[...]
    conv3 = (cw[0:1] * u_scr[:, CONV_HALO - 2:CONV_HALO - 2 + ts, :]
             + cw[1:2] * u_scr[:, CONV_HALO - 1:CONV_HALO - 1 + ts, :]
             + cw[2:3] * u_scr[:, CONV_HALO:CONV_HALO + ts, :])
    conv = cb_ref[...] + to2(conv3)
    yconv_ref[...] = proj(1) * conv * _silu(proj(3))
    cstate_ref[...] = u_scr[:, CONV_HALO + ts - 2:CONV_HALO + ts, :]

    H = 16
    pv = proj(4)
    e_scr[:, 0:1, :] = jnp.zeros((nb, 1, W_GROUP), F32)
    e_scr[:, 1:H, :] = spool_ref[...]
    e_scr[:, H:H + ts, :] = to3(pv)
    lane, wlane = _pool_lane_consts()
    wsum = jnp.zeros((nb, ts, W_GROUP), F32)
    for j in range(max(POOL_WINDOWS)):
        shifted = e_scr[:, H - j:H - j + ts, :]
        wsum = wsum + (shifted if j < min(POOL_WINDOWS) else jnp.where(wlane > j, shifted, 0.0))
    pos = start + lax.broadcasted_iota(jnp.int32, (ts, 1), 0)
    cnt = jnp.minimum(pos + 1, wlane).astype(F32)
    pooled = (to2(wsum / cnt) - pv).astype(BF16)
    pool_out = _nn(pooled, pw_ref[...]) * ps_ref[...]
    ypool_ref[...] = pool_out * _silu(proj(5))
    pstate_ref[...] = e_scr[:, H + ts - POOL_STATE:H + ts, :]

    q_ref[...] = _head_rms(proj(6), gq_ref[...], bd) * (HEAD_DIM ** -0.5)
    k_ref[...] = _head_rms(proj(7), gk_ref[...], bd)
    v_ref[...] = proj(8)
    az_ref[...] = _silu(proj(9))
    mq_ref[...] = _head_rms(proj(10), gmq_ref[...], bd) * (HEAD_DIM ** -0.5)
    mz_ref[...] = _silu(proj(11))


def _sample_front(x2, lw, sconv, spool, start):
    nb = sconv.shape[0]
    rows = x2.shape[0]
    ts = rows // nb
    full = lambda shape: pl.BlockSpec(shape, lambda i, n=len(shape): (0,) * n)
    t2 = jax.ShapeDtypeStruct((rows, W_GROUP), F32)
    return pl.pallas_call(
        functools.partial(_sample_front_kernel, start),
        grid=(1,),
        in_specs=[
            full((rows, D_MODEL)), full((1, D_MODEL)), full((D_MODEL, N_IN_SPLITS * W_GROUP)),
            full((CONV_W, W_GROUP)), full((1, W_GROUP)), full((W_GROUP, W_GROUP)),
            full((1, W_GROUP)), full((1, W_GROUP)), full((1, W_GROUP)), full((1, W_GROUP)),
            full((W_GROUP, W_GROUP)),
            full((nb, CONV_W - 1, W_GROUP)), full((nb, POOL_STATE, W_GROUP)),
        ],
        out_specs=[full((rows, W_GROUP))] * 8 + [full((nb, CONV_W - 1, W_GROUP)), full((nb, POOL_STATE, W_GROUP))],
        out_shape=[t2] * 8 + [jax.ShapeDtypeStruct((nb, CONV_W - 1, W_GROUP), F32),
                              jax.ShapeDtypeStruct((nb, POOL_STATE, W_GROUP), F32)],
        scratch_shapes=[
            pltpu.VMEM((nb, CONV_HALO + ts, W_GROUP), F32),
            pltpu.VMEM((nb, 16 + ts, W_GROUP), F32),
        ],
        compiler_params=pltpu.CompilerParams(dimension_semantics=("arbitrary",),
                                             vmem_limit_bytes=VMEM_LIMIT),
        name="sample_front",
    )(x2, lw["g_norm"], lw["w_in"], lw["conv_w"], lw["conv_b"], lw["pool_w"], lw["pool_scale"],
      lw["g_q"], lw["g_k"], lw["g_mq"], lw["bd"], sconv, spool)


def _head_rows(a, ts):
    r = lax.broadcasted_iota(jnp.int32, (N_HEADS * ts, W_GROUP), 0) // ts
    c = lax.broadcasted_iota(jnp.int32, (N_HEADS * ts, W_GROUP), 1) // HEAD_DIM
    return jnp.where(r == c, jnp.concatenate([a] * N_HEADS, axis=0), 0.0)


def _head_diag(o, ts):
    r = lax.broadcasted_iota(jnp.int32, (N_HEADS * ts, W_GROUP), 0) // ts
    c = lax.broadcasted_iota(jnp.int32, (N_HEADS * ts, W_GROUP), 1) // HEAD_DIM
    o = jnp.where(r == c, o, 0.0)
    out = o[0:ts]
    for hd in range(1, N_HEADS):
        out = out + o[hd * ts:(hd + 1) * ts]
    return out


def _sample_moba_kernel(past_len, pt_ref, q_ref, knew_ref, vnew_ref, mq_ref, az_ref, mz_ref, memkt_ref, memvt_ref,
                        *refs):
    P = PAGES_PER_STEP
    BPS = P // PAGES_PER_BLOCK
    kpages, vpages = refs[:P], refs[P:2 * P]
    yatt_ref, ymem_ref = refs[2 * P], refs[2 * P + 1]
    s_scr, p_scr, ksum_scr, acc_scr, l_scr, pown_scr = refs[2 * P + 2:]
    ph, c = pl.program_id(1), pl.program_id(2)
    nc = pl.num_programs(2)
    ts = q_ref.shape[0]
    R = N_HEADS * ts
    n_pages = past_len // PAGE_SIZE
    nblk = past_len // MOBA_BLOCK
    qrows = _head_rows(q_ref[...], ts)
    row_t = (lax.broadcasted_iota(jnp.int32, (R, 1), 0) % ts).astype(F32)
    row_h = lax.broadcasted_iota(jnp.int32, (R, 1), 0) // ts
    slope = jnp.where(row_h == 0, _slope(0), jnp.where(row_h == 1, _slope(1),
                                                        jnp.where(row_h == 2, _slope(2), _slope(3))))

    @pl.when(ph == 0)
    def _():
        qb = qrows.astype(BF16)
        lane8 = lax.broadcasted_iota(jnp.int32, (W_GROUP, BPS), 1)
        sums = jnp.zeros((W_GROUP, BPS), F32)
        for j in range(0, P, PAGES_PER_BLOCK):
            ka, kb = kpages[j][...], kpages[j + 1][...]
            s_scr[c * P + j] = _nn(qb, ka.astype(BF16))
            s_scr[c * P + j + 1] = _nn(qb, kb.astype(BF16))
            tot = jnp.sum(ka + kb, axis=1, keepdims=True)
            sums = jnp.where(lane8 == j // PAGES_PER_BLOCK, tot, sums)
        ksum_scr[c] = sums

    @pl.when((ph == 0) & (c == nc - 1))
    def _():
        gate = jnp.concatenate(
            [_dot3(qrows, ksum_scr[cc] * (1.0 / MOBA_BLOCK), _nn) for cc in range(n_pages // P)], axis=1)
        blk_iota = lax.broadcasted_iota(jnp.int32, (R, nblk), 1).astype(F32)
        sel = _top3_mask(gate, blk_iota, float(nblk), axis=-1)
        blk_term = jnp.where(sel > 0.0, slope * (blk_iota * MOBA_BLOCK - past_len), NEG_INF)
        col = lax.broadcasted_iota(jnp.int32, (R, PAGE_SIZE), 1).astype(F32)
        page_term = [slope * (col + float(k * PAGE_SIZE)) for k in range(PAGES_PER_BLOCK)]
        own = []
        m = jnp.full((R, 1), NEG_INF, F32)
        for j in range(ts):
            sj = jnp.sum(qrows * knew_ref[j:j + 1, :], axis=-1, keepdims=True) + slope * float(j)
            sj = jnp.where(row_t >= float(j), sj, NEG_INF)
            own.append(sj)
            m = jnp.maximum(m, sj)
        mvec = jnp.full((R, PAGE_SIZE), NEG_INF, F32)
        for n in range(nblk):
            for k in range(PAGES_PER_BLOCK):
                pg = n * PAGES_PER_BLOCK + k
                s = s_scr[pg] + page_term[k] + blk_term[:, n:n + 1]
                s_scr[pg] = s
                mvec = jnp.maximum(mvec, s)
        m = jnp.maximum(m, jnp.max(mvec, axis=-1, keepdims=True))
        l = jnp.zeros((R, 1), F32)
        for j in range(ts):
            pj = jnp.exp(own[j] - m)
            pown_scr[:, j:j + 1] = pj
            l = l + pj
        lvec = jnp.zeros((R, PAGE_SIZE), F32)
        for pg in range(n_pages):
            p = jnp.exp(s_scr[pg] - m)
            p_scr[pg] = p.astype(BF16)
            lvec = lvec + p
        l_scr[...] = l + jnp.sum(lvec, axis=-1, keepdims=True)

    @pl.when((ph == 1) & (c == 0))
    def _():
        acc_scr[...] = jnp.zeros(acc_scr.shape, F32)

    @pl.when(ph == 1)
    def _():
        acc = acc_scr[...]
        for j in range(P):
            acc = acc + _nt(p_scr[c * P + j], vpages[j][...].astype(BF16))
        acc_scr[...] = acc

    @pl.when((ph == 1) & (c == nc - 1))
    def _():
        acc = acc_scr[...]
        for j in range(ts):
            acc = acc + pown_scr[:, j:j + 1] * vnew_ref[j:j + 1, :]
        yatt_ref[...] = _head_diag(acc / l_scr[...], ts) * az_ref[...]
        mrows = _head_rows(mq_ref[...], ts).astype(BF16)
        s = _nn(mrows, memkt_ref[...].astype(BF16))
        p = jnp.exp(s - jnp.max(s, axis=-1, keepdims=True))
        lm = jnp.sum(p, axis=-1, keepdims=True)
        o = _nt(p.astype(BF16), memvt_ref[...].astype(BF16)) / lm
        ymem_ref[...] = _head_diag(o, ts) * mz_ref[...]


def _sample_moba(layer, page_table, q, knew, vnew, mq, az, mz, memkt, memvt, cache_kt, cache_vt):
    nb, n_pages = page_table.shape
    rows = q.shape[0]
    ts = rows // nb
    P = PAGES_PER_STEP
    nc = n_pages // P
    past_len = n_pages * PAGE_SIZE
    R = N_HEADS * ts
    row = lambda b, ph, c, pt: (b, 0)
    seq = pl.BlockSpec((ts, W_GROUP), row)
    mem = pl.BlockSpec((None, None, W_GROUP, N_MEM), lambda b, ph, c, pt: (layer, b, 0, 0))

    def kpage(j):
        return pl.BlockSpec((None, None, W_GROUP, PAGE_SIZE),
                            lambda b, ph, c, pt: (layer, pt[b, jnp.where(ph == 0, c, nc - 1) * P + j], 0, 0))

    def vpage(j):
        return pl.BlockSpec((None, None, W_GROUP, PAGE_SIZE),
                            lambda b, ph, c, pt: (layer, pt[b, jnp.where(ph == 0, 0, c) * P + j], 0, 0))

    out = jax.ShapeDtypeStruct((rows, W_GROUP), F32)
    grid_spec = pltpu.PrefetchScalarGridSpec(
        num_scalar_prefetch=1,
        grid=(nb, 2, nc),
        in_specs=[seq, seq, seq, seq, seq, seq, mem, mem] + [kpage(j) for j in range(P)] + [vpage(j) for j in range(P)],
        out_specs=[seq, seq],
        scratch_shapes=[
            pltpu.VMEM((n_pages, R, PAGE_SIZE), F32),
            pltpu.VMEM((n_pages, R, PAGE_SIZE), BF16),
            pltpu.VMEM((nc, W_GROUP, P // PAGES_PER_BLOCK), F32),
            pltpu.VMEM((R, W_GROUP), F32),
            pltpu.VMEM((R, 1), F32),
            pltpu.VMEM((R, ts), F32),
        ],
    )
    return pl.pallas_call(
        functools.partial(_sample_moba_kernel, past_len),
        grid_spec=grid_spec,
        out_shape=[out, out],
        compiler_params=pltpu.CompilerParams(dimension_semantics=("arbitrary", "arbitrary", "arbitrary"),
                                             vmem_limit_bytes=VMEM_LIMIT),
        name="sample_moba",
    )(page_table, q, knew, vnew, mq, az, mz, memkt, memvt, *([cache_kt] * P), *([cache_vt] * P))


def _same_head_matrix():
    r = lax.broadcasted_iota(jnp.int32, (W_GROUP, W_GROUP), 0) // HEAD_DIM
    c = lax.broadcasted_iota(jnp.int32, (W_GROUP, W_GROUP), 1) // HEAD_DIM
    return (r == c).astype(BF16)


def _pool_block_diag(pool_w_l):
    out = jnp.zeros((W_GROUP, W_GROUP), pool_w_l.dtype)
    for g in range(len(POOL_WINDOWS)):
        out = out.at[g * 64:(g + 1) * 64, g * 64:(g + 1) * 64].set(pool_w_l[g])
    return out


def _tokens_minor(a):
    lead = a.shape[:-3]
    n = len(lead)
    perm = tuple(range(n)) + (n + 1, n + 2, n)
    return a.transpose(perm).reshape(lead + (W_GROUP, a.shape[-3]))


def _tokens_major(a_t):
    lead = a_t.shape[:-2]
    n = len(lead)
    perm = tuple(range(n)) + (n + 2, n, n + 1)
    return a_t.reshape(lead + (N_HEADS, HEAD_DIM, a_t.shape[-1])).transpose(perm)


def kernel(x_prompt, x_sample, cache_k, cache_v, page_table, state_conv, state_pool, cache_mem_k, cache_mem_v,
           mem_prompt, g_norm, w_in, w_out, conv_w, conv_b, pool_w, pool_scale, g_q, g_k, g_mq, g_mk, g_mem,
           w_mem_kv):
    depth = w_in.shape[0]
    bp, seq, _ = x_prompt.shape
    bs, ts, _ = x_sample.shape
    n_pages = page_table.shape[1]
    past_len = n_pages * PAGE_SIZE
    bd = _same_head_matrix()
    tile4 = lambda g: jnp.tile(g, (1, N_HEADS))[:, None, :]

    w_in_bf = w_in.astype(BF16)
    w_out_bf = w_out.astype(BF16)
    g_q_t, g_k_t, g_mq_t, g_mk_t = tile4(g_q), tile4(g_k), tile4(g_mq), tile4(g_mk)
    layer_w = []
    for l in range(depth):
        layer_w.append(dict(
            g_norm=g_norm[l][None, :], w_in=w_in_bf[l], conv_w=conv_w[l], conv_b=conv_b[l][None, :],
            pool_w=_pool_block_diag(pool_w[l]).astype(BF16), pool_scale=pool_scale[l][None, :],
            g_q=g_q_t[l], g_k=g_k_t[l], g_mq=g_mq_t[l], bd=bd))

    mkt_all, mvt_all, mkt_bf, mv_bf = _memkv(mem_prompt, g_mem[:, None, :], w_mem_kv.astype(BF16), g_mk_t, bd)

    cache_kt = _tokens_minor(cache_k)
    cache_vt = _tokens_minor(cache_v)
    memkt_s = _tokens_minor(cache_mem_k)
    memvt_s = _tokens_minor(cache_mem_v)

    xp = x_prompt
    xs = x_sample.reshape(bs * ts, D_MODEL)
    kp_l, vp_l, ks_l, vs_l, cp_l, cs_l, pp_l, ps_l = ([] for _ in range(8))
    for l in range(depth):
        lw = layer_w[l]
        (q, kt, vt, kaug, vaug, kmean, yconv, ypool, ymem, az, cst, pst) = _prompt_front(xp, lw, mkt_bf[l], mv_bf[l])
        yatt = _prompt_moba(q, kaug, vaug, kmean.reshape(bp, seq // ROW_TILE, W_GROUP), az)
        flat = lambda a: a.reshape(bp * seq, W_GROUP)
        xp = _out_proj(xp.reshape(bp * seq, D_MODEL), flat(yconv), flat(ypool), flat(yatt), flat(ymem),
                       w_out_bf[l]).reshape(bp, seq, D_MODEL)
        kp_l.append(_tokens_major(kt))
        vp_l.append(_tokens_major(vt))
        cp_l.append(cst)
        pp_l.append(pst)
        (qs, ksn, vsn, mqs, azs, mzs, yconv_s, ypool_s, cst_s, pst_s) = _sample_front(
            xs, lw, state_conv[l], state_pool[l], past_len)
        yatt_s, ymem_s = _sample_moba(l, page_table, qs, ksn, vsn, mqs, azs, mzs, memkt_s, memvt_s,
                                      cache_kt, cache_vt)
        xs = _out_proj(xs, yconv_s, ypool_s, yatt_s, ymem_s, w_out_bf[l])
        ks_l.append(ksn.reshape(bs, ts, N_HEADS, HEAD_DIM))
        vs_l.append(vsn.reshape(bs, ts, N_HEADS, HEAD_DIM))
        cs_l.append(cst_s)
        ps_l.append(pst_s)

    return (xp, xs.reshape(bs, ts, D_MODEL), jnp.stack(kp_l), jnp.stack(vp_l), jnp.stack(ks_l), jnp.stack(vs_l),
            jnp.stack(cp_l), jnp.stack(cs_l), jnp.stack(pp_l), jnp.stack(ps_l),
            _tokens_major(mkt_all), _tokens_major(mvt_all))
```

```python
import functools

import jax
import jax.numpy as jnp
from jax import lax
from jax.experimental import pallas as pl
from jax.experimental.pallas import tpu as pltpu

F32 = jnp.float32
BF16 = jnp.bfloat16

D_MODEL = 1024
W_GROUP = 256
N_HEADS = 4
HEAD_DIM = 64
N_MEM = 256
CONV_W = 3
POOL_WINDOWS = (2, 4, 8, 16)
POOL_STATE = 15
MOBA_BLOCK = 256
MOBA_TOPK = 3
PAGE_SIZE = 128
N_IN_SPLITS = 12
EPS = 1e-6
NEG_INF = float("-inf")

ROW_TILE = 256
POOL_HALO = 32
CONV_HALO = 8
PAGES_PER_STEP = 16
PAGES_PER_BLOCK = MOBA_BLOCK // PAGE_SIZE
VMEM_LIMIT = 56 * 1024 * 1024

MAX_BLOCKS = 32
AUG_LANES = 64
K_GROUP_LANES = HEAD_DIM + AUG_LANES
K_AUG_LANES = N_HEADS * K_GROUP_LANES
V_AUG_ROWS = 80
MASKED = -1e30


def _slope(h):
    return 2.0 ** (-8.0 * (h + 1) / N_HEADS)


def _nt(a, b):
    return lax.dot_general(a, b, (((1,), (1,)), ((), ())), preferred_element_type=F32)


def _split_bf16(x):
    hi = x.astype(BF16)
    lo = (x - hi.astype(F32)).astype(BF16)
    return hi, lo


def _dot3(a, b, dot):
    ah, al = _split_bf16(a)
    bh, bl = _split_bf16(b)
    return dot(ah, bh) + dot(ah, bl) + dot(al, bh)


def _nn(a, b):
    return jnp.dot(a, b, preferred_element_type=F32)


def _rms(x, g):
    ms = jnp.mean(x * x, axis=-1, keepdims=True)
    return x * lax.rsqrt(ms + EPS) * g


def _head_rms(x, g, bd):
    hi, lo = _split_bf16(x * x)
    ssq = _nn(hi, bd) + _nn(lo, bd)
    return x * lax.rsqrt(ssq * (1.0 / HEAD_DIM) + EPS) * g


def _silu(z):
    return z / (1.0 + jnp.exp(-z))


def _top3_mask(gate, blk_iota, limit, axis):
    sel = jnp.zeros(gate.shape, F32)
    g = gate
    for _ in range(MOBA_TOPK):
        m = jnp.max(g, axis=axis, keepdims=True)
        idx = jnp.min(jnp.where(g == m, blk_iota, 1e9), axis=axis, keepdims=True)
        pick = blk_iota == idx
        sel = jnp.where(pick & (idx < limit), 1.0, sel)
        g = jnp.where(pick, NEG_INF, g)
    return sel


def _pool_lane_consts():
    lane = lax.broadcasted_iota(jnp.int32, (1, W_GROUP), 1)
    w = jnp.where(lane < 64, 2, jnp.where(lane < 128, 4, jnp.where(lane < 192, 8, 16)))
    return lane, w


def _memkv_kernel(mem_ref, g_ref, w_ref, gmk_ref, bd_ref, mkt_ref, mvt_ref, mktb_ref, mvb_ref):
    h = _rms(mem_ref[...], g_ref[...]).astype(BF16)
    kv = _nn(h, w_ref[...])
    mkt = _head_rms(kv[:, :W_GROUP], gmk_ref[...], bd_ref[...]).T
    mv = kv[:, W_GROUP:]
    mkt_ref[...] = mkt
    mvt_ref[...] = mv.T
    mktb_ref[...] = mkt.astype(BF16)
    mvb_ref[...] = mv.astype(BF16)


def _memkv(mem_prompt, g_mem, w_mem_kv_bf, g_mk_t, bd):
    depth = g_mem.shape[0]
    batch = mem_prompt.shape[0]
    out = lambda dt: jax.ShapeDtypeStruct((depth, batch, N_MEM, W_GROUP), dt)
    ospec = pl.BlockSpec((None, None, N_MEM, W_GROUP), lambda l, b: (l, b, 0, 0))
    return pl.pallas_call(
        _memkv_kernel,
        grid=(depth, batch),
        in_specs=[
            pl.BlockSpec((None, N_MEM, D_MODEL), lambda l, b: (b, 0, 0)),
            pl.BlockSpec((None, 1, D_MODEL), lambda l, b: (l, 0, 0)),
            pl.BlockSpec((None, D_MODEL, 2 * W_GROUP), lambda l, b: (l, 0, 0)),
            pl.BlockSpec((None, 1, W_GROUP), lambda l, b: (l, 0, 0)),
            pl.BlockSpec((W_GROUP, W_GROUP), lambda l, b: (0, 0)),
        ],
        out_specs=[ospec, ospec, ospec, ospec],
        out_shape=[out(F32), out(F32), out(BF16), out(BF16)],
        compiler_params=pltpu.CompilerParams(dimension_semantics=("arbitrary", "arbitrary")),
        name="memkv",
    )(mem_prompt, g_mem, w_mem_kv_bf, g_mk_t, bd)


def _prompt_front_kernel(x_ref, gn_ref, win_ref, cw_ref, cb_ref, pw_ref, ps_ref, gq_ref, gk_ref, gmq_ref,
                         bd_ref, memkt_ref, memv_ref,
                         q_ref, kt_ref, vt_ref, kaug_ref, vaug_ref, kmean_ref,
                         yconv_ref, ypool_ref, ymem_ref, az_ref, cstate_ref, pstate_ref,
                         u_scr, e_scr, s2_scr, s4_scr, s8_scr, proj_scr):
    t = pl.program_id(1)
    T = ROW_TILE

    @pl.when(t == 0)
    def _():
        u_scr[0:CONV_HALO, :] = jnp.zeros((CONV_HALO, W_GROUP), F32)
        e_scr[0:POOL_HALO, :] = jnp.zeros((POOL_HALO, W_GROUP), F32)

    h = _rms(x_ref[...], gn_ref[...]).astype(BF16)
    proj_scr[...] = _nn(h, win_ref[...])
    bd = bd_ref[...]

    def proj(g):
        return proj_scr[:, g * W_GROUP:(g + 1) * W_GROUP]

    u = proj(2) * proj(0)
    u_scr[CONV_HALO:CONV_HALO + T, :] = u
    cw = cw_ref[...]
    conv = (cb_ref[...] + cw[0:1] * u_scr[CONV_HALO - 2:CONV_HALO - 2 + T, :]
            + cw[1:2] * u_scr[CONV_HALO - 1:CONV_HALO - 1 + T, :] + cw[2:3] * u)
    yconv_ref[...] = (proj(1) * conv * _silu(proj(3))).astype(BF16)
    cstate_ref[...] = u_scr[CONV_HALO + T - 2:CONV_HALO + T, :]
    u_scr[0:CONV_HALO, :] = u_scr[T:T + CONV_HALO, :]

    H = POOL_HALO
    pv = proj(4)
    e_scr[H:H + T, :] = pv
    s2_scr[8:H + T, :] = e_scr[8:H + T, :] + e_scr[7:H + T - 1, :]
    s4_scr[16:H + T, :] = s2_scr[16:H + T, :] + s2_scr[14:H + T - 2, :]
    s8_scr[24:H + T, :] = s4_scr[24:H + T, :] + s4_scr[20:H + T - 4, :]
    s16 = s8_scr[H:H + T, :] + s8_scr[H - 8:H + T - 8, :]
    lane, wlane = _pool_lane_consts()
    wsum = jnp.where(lane < 64, s2_scr[H:H + T, :],
                     jnp.where(lane < 128, s4_scr[H:H + T, :],
                               jnp.where(lane < 192, s8_scr[H:H + T, :], s16)))
    pos = t * T + lax.broadcasted_iota(jnp.int32, (T, 1), 0)
    cnt = jnp.minimum(pos + 1, wlane).astype(F32)
    pooled = (wsum / cnt - pv).astype(BF16)
    pool_out = _nn(pooled, pw_ref[...]) * ps_ref[...]
    ypool_ref[...] = (pool_out * _silu(proj(5))).astype(BF16)
    pstate_ref[...] = e_scr[H + T - POOL_STATE:H + T, :]
    e_scr[0:H, :] = e_scr[T:T + H, :]

    q_ref[...] = _head_rms(proj(6), gq_ref[...], bd) * (HEAD_DIM ** -0.5)
    k = _head_rms(proj(7), gk_ref[...], bd)
    v = proj(8)
    vt = v.T
    kt_ref[...] = k.T
    vt_ref[...] = vt
    kmean_ref[...] = jnp.mean(k, axis=0, keepdims=True)
    lane64 = lax.broadcasted_iota(jnp.int32, (T, AUG_LANES), 1)
    key_idx = lax.broadcasted_iota(jnp.int32, (T, AUG_LANES), 0).astype(F32)
    onehot = (lane64 == t).astype(F32)
    ones_row = (lax.broadcasted_iota(jnp.int32, (V_AUG_ROWS - HEAD_DIM, T), 0) == 0).astype(F32)
    kparts, vparts = [], []
    for hd in range(N_HEADS):
        hs = slice(hd * HEAD_DIM, (hd + 1) * HEAD_DIM)
        kparts += [k[:, hs], jnp.where(lane64 == MAX_BLOCKS, _slope(hd) * key_idx, onehot)]
        vparts += [vt[hs, :], ones_row]
    kaug_ref[...] = jnp.concatenate(kparts, axis=1).astype(BF16)
    vaug_ref[...] = jnp.concatenate(vparts, axis=0).astype(BF16)
    az_ref[...] = _silu(proj(9)).astype(BF16)

    mq = (_head_rms(proj(10), gmq_ref[...], bd) * (HEAD_DIM ** -0.5)).astype(BF16)
    mz = _silu(proj(11))
    for hd in range(N_HEADS):
        hs = slice(hd * HEAD_DIM, (hd + 1) * HEAD_DIM)
        s = _nn(mq[:, hs], memkt_ref[hs, :])
        p = jnp.exp(s - jnp.max(s, axis=-1, keepdims=True))
        l = jnp.sum(p, axis=-1, keepdims=True)
        o = _nn(p.astype(BF16), memv_ref[:, hs]) / l
        ymem_ref[:, hs] = (o * mz[:, hs]).astype(BF16)


def _prompt_front(x, lw, memkt_bf, memv_bf):
    batch, seq, _ = x.shape
    nt = seq // ROW_TILE
    row = lambda b, t: (b, t, 0)
    const2 = lambda b, t: (0, 0)
    tile = lambda dt: jax.ShapeDtypeStruct((batch, seq, W_GROUP), dt)
    tile_t = jax.ShapeDtypeStruct((batch, W_GROUP, seq), F32)
    assert nt <= MAX_BLOCKS
    tile_spec = pl.BlockSpec((None, ROW_TILE, W_GROUP), row)
    tile_t_spec = pl.BlockSpec((None, W_GROUP, ROW_TILE), lambda b, t: (b, 0, t))
    kaug_spec = pl.BlockSpec((None, None, ROW_TILE, K_AUG_LANES), lambda b, t: (b, t, 0, 0))
    vaug_spec = pl.BlockSpec((None, None, N_HEADS * V_AUG_ROWS, ROW_TILE), lambda b, t: (b, t, 0, 0))
    vec = pl.BlockSpec((1, W_GROUP), const2)
    return pl.pallas_call(
        _prompt_front_kernel,
        grid=(batch, nt),
        in_specs=[
            pl.BlockSpec((None, ROW_TILE, D_MODEL), row),
            pl.BlockSpec((1, D_MODEL), const2),
            pl.BlockSpec((D_MODEL, N_IN_SPLITS * W_GROUP), const2),
            pl.BlockSpec((CONV_W, W_GROUP), const2),
            vec,
            pl.BlockSpec((W_GROUP, W_GROUP), const2),
            vec, vec, vec, vec,
            pl.BlockSpec((W_GROUP, W_GROUP), const2),
            pl.BlockSpec((None, W_GROUP, N_MEM), lambda b, t: (b, 0, 0)),
            pl.BlockSpec((None, N_MEM, W_GROUP), lambda b, t: (b, 0, 0)),
        ],
        out_specs=[
            tile_spec, tile_t_spec, tile_t_spec, kaug_spec, vaug_spec,
            pl.BlockSpec((None, None, 1, W_GROUP), lambda b, t: (b, t, 0, 0)),
            tile_spec, tile_spec, tile_spec, tile_spec,
            pl.BlockSpec((None, CONV_W - 1, W_GROUP), lambda b, t: (b, 0, 0)),
            pl.BlockSpec((None, POOL_STATE, W_GROUP), lambda b, t: (b, 0, 0)),
        ],
        out_shape=[
            tile(F32), tile_t, tile_t,
            jax.ShapeDtypeStruct((batch, nt, ROW_TILE, K_AUG_LANES), BF16),
            jax.ShapeDtypeStruct((batch, nt, N_HEADS * V_AUG_ROWS, ROW_TILE), BF16),
            jax.ShapeDtypeStruct((batch, nt, 1, W_GROUP), F32),
            tile(BF16), tile(BF16), tile(BF16), tile(BF16),
            jax.ShapeDtypeStruct((batch, CONV_W - 1, W_GROUP), F32),
            jax.ShapeDtypeStruct((batch, POOL_STATE, W_GROUP), F32),
        ],
        scratch_shapes=[
            pltpu.VMEM((CONV_HALO + ROW_TILE, W_GROUP), F32),
            pltpu.VMEM((POOL_HALO + ROW_TILE, W_GROUP), F32),
            pltpu.VMEM((POOL_HALO + ROW_TILE, W_GROUP), F32),
            pltpu.VMEM((POOL_HALO + ROW_TILE, W_GROUP), F32),
            pltpu.VMEM((POOL_HALO + ROW_TILE, W_GROUP), F32),
            pltpu.VMEM((ROW_TILE, N_IN_SPLITS * W_GROUP), F32),
        ],
        compiler_params=pltpu.CompilerParams(dimension_semantics=("arbitrary", "arbitrary"),
                                             vmem_limit_bytes=VMEM_LIMIT),
        name="prompt_front",
    )(x, lw["g_norm"], lw["w_in"], lw["conv_w"], lw["conv_b"], lw["pool_w"], lw["pool_scale"],
      lw["g_q"], lw["g_k"], lw["g_mq"], lw["bd"], memkt_bf, memv_bf)


def _prompt_moba_kernel(q_ref, kaug_ref, vaug_ref, kmean_ref, az_ref, x_ref, yconv_ref, ypool_ref, ymem_ref, wout_ref,
                        y_ref, qaug_scr, acc_scr, outt_scr, s_scr):
    i = pl.program_id(1)
    T = ROW_TILE
    nb = kmean_ref.shape[0]
    qt = q_ref[...].T
    km = kmean_ref[...]
    blk_iota = lax.broadcasted_iota(jnp.int32, (nb, T), 0)
    blk_f = blk_iota.astype(F32)
    i_f = i.astype(F32)
    ones_rows = (lax.broadcasted_iota(jnp.int32, (8, T), 0) == 0).astype(F32)
    pad_rows = jnp.zeros((K_GROUP_LANES - HEAD_DIM - MAX_BLOCKS - 8, T), F32)
    causal = lax.broadcasted_iota(jnp.int32, (T, T), 0) <= lax.broadcasted_iota(jnp.int32, (T, T), 1)

    def kgroup(n, hd):
        return kaug_ref[n, :, hd * K_GROUP_LANES:(hd + 1) * K_GROUP_LANES]

    def vgroup(n, hd):
        return vaug_ref[n, hd * V_AUG_ROWS:(hd + 1) * V_AUG_ROWS, :]

    m0, mx0 = [], []
    for hd in range(N_HEADS):
        hs = slice(hd * HEAD_DIM, (hd + 1) * HEAD_DIM)
        qt_h = qt[hs, :]
        gate = jnp.where(blk_f < i_f, _dot3(km[:, hs], qt_h, _nn), NEG_INF)
        sel = _top3_mask(gate, blk_f, i_f, axis=0)
        bias = jnp.where((sel > 0.0) | (blk_iota == i), 0.0, MASKED)
        if nb < MAX_BLOCKS:
            bias = jnp.concatenate([bias, jnp.zeros((MAX_BLOCKS - nb, T), F32)], axis=0)
        qaug_scr[hd] = jnp.concatenate([qt_h, bias, ones_rows, pad_rows], axis=0).astype(BF16)
        s = jnp.where(causal, _nn(kgroup(i, hd), qaug_scr[hd]), NEG_INF)
        m = jnp.max(s, axis=0, keepdims=True)
        acc_scr[hd] = _nn(vgroup(i, hd), jnp.exp(s - m).astype(BF16))
        m0.append(m)
        s = _nn(kgroup(0, hd), qaug_scr[hd])
        s_scr[0, hd] = s
        mx0.append(jnp.max(s, axis=0, keepdims=True))

    def stage(n, cur, nxt, ms, mxs):
        n_next = jnp.minimum(n + 1, i)
        ms_out, mxs_out = [], []
        for hd in range(N_HEADS):
            s_next = _nn(kgroup(n_next, hd), qaug_scr[hd])
            s_scr[nxt, hd] = s_next
            mxs_out.append(jnp.max(s_next, axis=0, keepdims=True))
            cn = jnp.where(n < i, (n - i).astype(F32) * (_slope(hd) * MOBA_BLOCK), NEG_INF)
            m_new = jnp.maximum(ms[hd], mxs[hd] + cn)
            alpha = jnp.exp(ms[hd] - m_new)
            p = jnp.exp(s_scr[cur, hd] - (m_new - cn))
            acc_scr[hd] = alpha * acc_scr[hd] + _nn(vgroup(jnp.minimum(n, i), hd), p.astype(BF16))
            ms_out.append(m_new)
        return tuple(ms_out), tuple(mxs_out)

    def body(j, carry):
        ms, mxs = stage(2 * j, 0, 1, *carry)
        return stage(2 * j + 1, 1, 0, ms, mxs)

    lax.fori_loop(0, (i + 1) // 2, body, (tuple(m0), tuple(mx0)))
    for hd in range(N_HEADS):
        acc = acc_scr[hd]
        outt_scr[hd * HEAD_DIM:(hd + 1) * HEAD_DIM, :] = acc[0:HEAD_DIM] / acc[HEAD_DIM:HEAD_DIM + 1]
    yatt = (outt_scr[...].T * az_ref[...].astype(F32)).astype(BF16)
    y = x_ref[...]
    for g, part in enumerate((yconv_ref[...], ypool_ref[...], yatt, ymem_ref[...])):
        y = y + _nn(part, wout_ref[g * W_GROUP:(g + 1) * W_GROUP, :])
    y_ref[...] = y


def _prompt_moba(q, kaug, vaug, kmean, az, x, yconv, ypool, ymem, w_out_bf):
    batch, seq, _ = q.shape
    nt = seq // ROW_TILE
    row = lambda b, t: (b, t, 0)
    part = pl.BlockSpec((None, ROW_TILE, W_GROUP), row)
    return pl.pallas_call(
        _prompt_moba_kernel,
        grid=(batch, nt),
        in_specs=[
            part,
            pl.BlockSpec((None, nt, ROW_TILE, K_AUG_LANES), lambda b, t: (b, 0, 0, 0)),
            pl.BlockSpec((None, nt, N_HEADS * V_AUG_ROWS, ROW_TILE), lambda b, t: (b, 0, 0, 0)),
            pl.BlockSpec((None, nt, W_GROUP), lambda b, t: (b, 0, 0)),
            part,
            pl.BlockSpec((None, ROW_TILE, D_MODEL), row),
            part, part, part,
            pl.BlockSpec((D_MODEL, D_MODEL), lambda b, t: (0, 0)),
        ],
        out_specs=pl.BlockSpec((None, ROW_TILE, D_MODEL), row),
        out_shape=jax.ShapeDtypeStruct((batch, seq, D_MODEL), F32),
        scratch_shapes=[
            pltpu.VMEM((N_HEADS, K_GROUP_LANES, ROW_TILE), BF16),
            pltpu.VMEM((N_HEADS, V_AUG_ROWS, ROW_TILE), F32),
            pltpu.VMEM((W_GROUP, ROW_TILE), F32),
            pltpu.VMEM((2, N_HEADS, MOBA_BLOCK, ROW_TILE), F32),
        ],
        compiler_params=pltpu.CompilerParams(dimension_semantics=("arbitrary", "arbitrary"),
                                             vmem_limit_bytes=VMEM_LIMIT),
        name="prompt_moba",
    )(q, kaug, vaug, kmean, az, x, yconv, ypool, ymem, w_out_bf)


def _out_kernel(x_ref, yconv_ref, ypool_ref, yatt_ref, ymem_ref, w_ref, y_ref):
    acc = x_ref[...]
    for g, ref in enumerate((yconv_ref, ypool_ref, yatt_ref, ymem_ref)):
        acc = acc + _nn(ref[...].astype(BF16), w_ref[g * W_GROUP:(g + 1) * W_GROUP, :])
    y_ref[...] = acc


def _out_proj(x2, yconv, ypool, yatt, ymem, w_out_bf):
    rows = x2.shape[0]
    row = lambda r: (r, 0)
    part = pl.BlockSpec((ROW_TILE, W_GROUP), row)
    return pl.pallas_call(
        _out_kernel,
        grid=(rows // ROW_TILE,),
        in_specs=[pl.BlockSpec((ROW_TILE, D_MODEL), row), part, part, part, part,
                  pl.BlockSpec((D_MODEL, D_MODEL), lambda r: (0, 0))],
        out_specs=pl.BlockSpec((ROW_TILE, D_MODEL), row),
        out_shape=jax.ShapeDtypeStruct((rows, D_MODEL), F32),
        compiler_params=pltpu.CompilerParams(dimension_semantics=("arbitrary",),
                                             vmem_limit_bytes=VMEM_LIMIT),
        name="out_proj",
    )(x2, yconv, ypool, yatt, ymem, w_out_bf)


def _sample_front_kernel(start, x_ref, gn_ref, win_ref, cw_ref, cb_ref, pw_ref, ps_ref, gq_ref, gk_ref, gmq_ref,
                         bd_ref, sconv_ref, spool_ref,
                         q_ref, k_ref, v_ref, mq_ref, az_ref, mz_ref, yconv_ref, ypool_ref, cstate_ref, pstate_ref,
                         u_scr, e_scr):
    nb, ts = sconv_ref.shape[0], u_scr.shape[1] - CONV_HALO
    rows = nb * ts
    h = _rms(x_ref[...], gn_ref[...]).astype(BF16)
    bd = bd_ref[...]

    def proj(g):
        return _nn(h, win_ref[:, g * W_GROUP:(g + 1) * W_GROUP])

    def to3(a):
        return a.reshape(nb, ts, W_GROUP)

    def to2(a):
        return a.reshape(rows, W_GROUP)

    u = proj(2) * proj(0)
    u_scr[:, 0:CONV_HALO - 2, :] = jnp.zeros((nb, CONV_HALO - 2, W_GROUP), F32)
    u_scr[:, CONV_HALO - 2:CONV_HALO, :] = sconv_ref[...]
    u_scr[:, CONV_HALO:CONV_HALO + ts, :] = to3(u)
    cw = cw_ref[...]
    conv3 = (cw[0:1] * u_scr[:, CONV_HALO - 2:CONV_HALO - 2 + ts, :]
             + cw[1:2] * u_scr[:, CONV_HALO - 1:CONV_HALO - 1 + ts, :]
             + cw[2:3] * u_scr[:, CONV_HALO:CONV_HALO + ts, :])
    conv = cb_ref[...] + to2(conv3)
    yconv_ref[...] = proj(1) * conv * _silu(proj(3))
    cstate_ref[...] = u_scr[:, CONV_HALO + ts - 2:CONV_HALO + ts, :]

    H = 16
    pv = proj(4)
    e_scr[:, 0:1, :] = jnp.zeros((nb, 1, W_GROUP), F32)
    e_scr[:, 1:H, :] = spool_ref[...]
    e_scr[:, H:H + ts, :] = to3(pv)
    lane, wlane = _pool_lane_consts()
    wsum = jnp.zeros((nb, ts, W_GROUP), F32)
    for j in range(max(POOL_WINDOWS)):
        shifted = e_scr[:, H - j:H - j + ts, :]
        wsum = wsum + (shifted if j < min(POOL_WINDOWS) else jnp.where(wlane > j, shifted, 0.0))
    pos = start + lax.broadcasted_iota(jnp.int32, (ts, 1), 0)
    cnt = jnp.minimum(pos + 1, wlane).astype(F32)
    pooled = (to2(wsum / cnt) - pv).astype(BF16)
    pool_out = _nn(pooled, pw_ref[...]) * ps_ref[...]
    ypool_ref[...] = pool_out * _silu(proj(5))
    pstate_ref[...] = e_scr[:, H + ts - POOL_STATE:H + ts, :]

    q_ref[...] = _head_rms(proj(6), gq_ref[...], bd) * (HEAD_DIM ** -0.5)
    k_ref[...] = _head_rms(proj(7), gk_ref[...], bd)
    v_ref[...] = proj(8)
    az_ref[...] = _silu(proj(9))
    mq_ref[...] = _head_rms(proj(10), gmq_ref[...], bd) * (HEAD_DIM ** -0.5)
    mz_ref[...] = _silu(proj(11))


def _sample_front(x2, lw, sconv, spool, start):
    nb = sconv.shape[0]
    rows = x2.shape[0]
    ts = rows // nb
    full = lambda shape: pl.BlockSpec(shape, lambda i, n=len(shape): (0,) * n)
    t2 = jax.ShapeDtypeStruct((rows, W_GROUP), F32)
    return pl.pallas_call(
        functools.partial(_sample_front_kernel, start),
        grid=(1,),
        in_specs=[
            full((rows, D_MODEL)), full((1, D_MODEL)), full((D_MODEL, N_IN_SPLITS * W_GROUP)),
            full((CONV_W, W_GROUP)), full((1, W_GROUP)), full((W_GROUP, W_GROUP)),
            full((1, W_GROUP)), full((1, W_GROUP)), full((1, W_GROUP)), full((1, W_GROUP)),
            full((W_GROUP, W_GROUP)),
            full((nb, CONV_W - 1, W_GROUP)), full((nb, POOL_STATE, W_GROUP)),
        ],
        out_specs=[full((rows, W_GROUP))] * 8 + [full((nb, CONV_W - 1, W_GROUP)), full((nb, POOL_STATE, W_GROUP))],
        out_shape=[t2] * 8 + [jax.ShapeDtypeStruct((nb, CONV_W - 1, W_GROUP), F32),
                              jax.ShapeDtypeStruct((nb, POOL_STATE, W_GROUP), F32)],
        scratch_shapes=[
            pltpu.VMEM((nb, CONV_HALO + ts, W_GROUP), F32),
            pltpu.VMEM((nb, 16 + ts, W_GROUP), F32),
        ],
        compiler_params=pltpu.CompilerParams(dimension_semantics=("arbitrary",),
                                             vmem_limit_bytes=VMEM_LIMIT),
        name="sample_front",
    )(x2, lw["g_norm"], lw["w_in"], lw["conv_w"], lw["conv_b"], lw["pool_w"], lw["pool_scale"],
      lw["g_q"], lw["g_k"], lw["g_mq"], lw["bd"], sconv, spool)


def _head_rows(a, ts):
    r = lax.broadcasted_iota(jnp.int32, (N_HEADS * ts, W_GROUP), 0) // ts
    c = lax.broadcasted_iota(jnp.int32, (N_HEADS * ts, W_GROUP), 1) // HEAD_DIM
    return jnp.where(r == c, jnp.concatenate([a] * N_HEADS, axis=0), 0.0)


def _head_diag(o, ts):
    r = lax.broadcasted_iota(jnp.int32, (N_HEADS * ts, W_GROUP), 0) // ts
    c = lax.broadcasted_iota(jnp.int32, (N_HEADS * ts, W_GROUP), 1) // HEAD_DIM
    o = jnp.where(r == c, o, 0.0)
    out = o[0:ts]
    for hd in range(1, N_HEADS):
        out = out + o[hd * ts:(hd + 1) * ts]
    return out


def _sample_moba_kernel(layer, past_len, pt_ref, q_ref, knew_ref, vnew_ref, mq_ref, az_ref, mz_ref, memkt_ref,
                        memvt_ref, ck_hbm, cv_hbm, yatt_ref, ymem_ref, buf, sem, s_scr, p_scr):
    P = PAGES_PER_STEP
    BPS = P // PAGES_PER_BLOCK
    b = pl.program_id(0)
    n_seqs = pl.num_programs(0)
    ts = q_ref.shape[0]
    R = N_HEADS * ts
    n_pages = past_len // PAGE_SIZE
    nblk = past_len // MOBA_BLOCK
    nc = n_pages // P
    n_chunks = 2 * nc

    def chunk_copies(seq, g):
        src = ck_hbm if g < nc else cv_hbm
        first = (g % nc) * P
        return [pltpu.make_async_copy(src.at[layer, pt_ref[seq, first + j]], buf.at[g % 2, j], sem.at[g % 2])
                for j in range(P)]

    def start(seq, g):
        for cp in chunk_copies(seq, g):
            cp.start()

    def wait(seq, g):
        for cp in chunk_copies(seq, g):
            cp.wait()

    def prefetch_after(g):
        if g + 1 < n_chunks:
            start(b, g + 1)
        else:
            @pl.when(b + 1 < n_seqs)
            def _():
                start(b + 1, 0)

    @pl.when(b == 0)
    def _():
        start(b, 0)

    qrows = _head_rows(q_ref[...], ts)
    row_t = (lax.broadcasted_iota(jnp.int32, (R, 1), 0) % ts).astype(F32)
    row_h = lax.broadcasted_iota(jnp.int32, (R, 1), 0) // ts
    slope = jnp.where(row_h == 0, _slope(0), jnp.where(row_h == 1, _slope(1),
                                                        jnp.where(row_h == 2, _slope(2), _slope(3))))

    qb = qrows.astype(BF16)
    lane8 = lax.broadcasted_iota(jnp.int32, (W_GROUP, BPS), 1)
    ksums = []
    for g in range(nc):
        prefetch_after(g)
        wait(b, g)
        sums = jnp.zeros((W_GROUP, BPS), F32)
        for j in range(0, P, PAGES_PER_BLOCK):
            ka, kb = buf[g % 2, j], buf[g % 2, j + 1]
            s_scr[g * P + j] = _nn(qb, ka.astype(BF16))
            s_scr[g * P + j + 1] = _nn(qb, kb.astype(BF16))
            tot = jnp.sum(ka + kb, axis=1, keepdims=True)
            sums = jnp.where(lane8 == j // PAGES_PER_BLOCK, tot, sums)
        ksums.append(sums)

    gate = jnp.concatenate([_dot3(qrows, ks * (1.0 / MOBA_BLOCK), _nn) for ks in ksums], axis=1)
    blk_iota = lax.broadcasted_iota(jnp.int32, (R, nblk), 1).astype(F32)
    sel = _top3_mask(gate, blk_iota, float(nblk), axis=-1)
    blk_term = jnp.where(sel > 0.0, slope * (blk_iota * MOBA_BLOCK - past_len), NEG_INF)
    col = lax.broadcasted_iota(jnp.int32, (R, PAGE_SIZE), 1).astype(F32)
    page_term = [slope * (col + float(k * PAGE_SIZE)) for k in range(PAGES_PER_BLOCK)]
    own = []
    m = jnp.full((R, 1), NEG_INF, F32)
    for j in range(ts):
        sj = jnp.sum(qrows * knew_ref[j:j + 1, :], axis=-1, keepdims=True) + slope * float(j)
        sj = jnp.where(row_t >= float(j), sj, NEG_INF)
        own.append(sj)
        m = jnp.maximum(m, sj)
    mvec = jnp.full((R, PAGE_SIZE), NEG_INF, F32)
    for n in range(nblk):
        for k in range(PAGES_PER_BLOCK):
            pg = n * PAGES_PER_BLOCK + k
            s = s_scr[pg] + page_term[k] + blk_term[:, n:n + 1]
            s_scr[pg] = s
            mvec = jnp.maximum(mvec, s)
    m = jnp.maximum(m, jnp.max(mvec, axis=-1, keepdims=True))
    p_own = [jnp.exp(sj - m) for sj in own]
    lvec = jnp.zeros((R, PAGE_SIZE), F32)
    for pg in range(n_pages):
        p = jnp.exp(s_scr[pg] - m)
        p_scr[pg] = p.astype(BF16)
        lvec = lvec + p
    l = sum(p_own) + jnp.sum(lvec, axis=-1, keepdims=True)

    acc = jnp.zeros((R, W_GROUP), F32)
    for g in range(nc, n_chunks):
        prefetch_after(g)
        wait(b, g)
        for j in range(P):
            acc = acc + _nt(p_scr[(g - nc) * P + j], buf[g % 2, j].astype(BF16))
    for j in range(ts):
        acc = acc + p_own[j] * vnew_ref[j:j + 1, :]
    yatt_ref[...] = _head_diag(acc / l, ts) * az_ref[...]

    mrows = _head_rows(mq_ref[...], ts).astype(BF16)
    s = _nn(mrows, memkt_ref[...].astype(BF16))
    p = jnp.exp(s - jnp.max(s, axis=-1, keepdims=True))
    lm = jnp.sum(p, axis=-1, keepdims=True)
    o = _nt(p.astype(BF16), memvt_ref[...].astype(BF16)) / lm
    ymem_ref[...] = _head_diag(o, ts) * mz_ref[...]


def _sample_moba(layer, page_table, q, knew, vnew, mq, az, mz, memkt, memvt, cache_kt, cache_vt):
    nb, n_pages = page_table.shape
    rows = q.shape[0]
    ts = rows // nb
    P = PAGES_PER_STEP
    assert n_pages % P == 0 and P % PAGES_PER_BLOCK == 0
    past_len = n_pages * PAGE_SIZE
    R = N_HEADS * ts
    seq = pl.BlockSpec((ts, W_GROUP), lambda b, pt: (b, 0))
    mem = pl.BlockSpec((None, None, W_GROUP, N_MEM), lambda b, pt: (layer, b, 0, 0))
    hbm = pl.BlockSpec(memory_space=pl.ANY)
    out = jax.ShapeDtypeStruct((rows, W_GROUP), F32)
    grid_spec = pltpu.PrefetchScalarGridSpec(
        num_scalar_prefetch=1,
        grid=(nb,),
        in_specs=[seq, seq, seq, seq, seq, seq, mem, mem, hbm, hbm],
        out_specs=[seq, seq],
        scratch_shapes=[
            pltpu.VMEM((2, P, W_GROUP, PAGE_SIZE), F32),
            pltpu.SemaphoreType.DMA((2,)),
            pltpu.VMEM((n_pages, R, PAGE_SIZE), F32),
            pltpu.VMEM((n_pages, R, PAGE_SIZE), BF16),
        ],
    )
    return pl.pallas_call(
        functools.partial(_sample_moba_kernel, layer, past_len),
        grid_spec=grid_spec,
        out_shape=[out, out],
        compiler_params=pltpu.CompilerParams(dimension_semantics=("arbitrary",),
                                             vmem_limit_bytes=VMEM_LIMIT),
        name="sample_moba",
    )(page_table, q, knew, vnew, mq, az, mz, memkt, memvt, cache_kt, cache_vt)


def _same_head_matrix():
    r = lax.broadcasted_iota(jnp.int32, (W_GROUP, W_GROUP), 0) // HEAD_DIM
    c = lax.broadcasted_iota(jnp.int32, (W_GROUP, W_GROUP), 1) // HEAD_DIM
    return (r == c).astype(BF16)


def _pool_block_diag(pool_w_l):
    out = jnp.zeros((W_GROUP, W_GROUP), pool_w_l.dtype)
    for g in range(len(POOL_WINDOWS)):
        out = out.at[g * 64:(g + 1) * 64, g * 64:(g + 1) * 64].set(pool_w_l[g])
    return out


def _tokens_minor(a):
    lead = a.shape[:-3]
    n = len(lead)
    perm = tuple(range(n)) + (n + 1, n + 2, n)
    return a.transpose(perm).reshape(lead + (W_GROUP, a.shape[-3]))


def _tokens_major(a_t):
    lead = a_t.shape[:-2]
    n = len(lead)
    perm = tuple(range(n)) + (n + 2, n, n + 1)
    return a_t.reshape(lead + (N_HEADS, HEAD_DIM, a_t.shape[-1])).transpose(perm)


def kernel(x_prompt, x_sample, cache_k, cache_v, page_table, state_conv, state_pool, cache_mem_k, cache_mem_v,
           mem_prompt, g_norm, w_in, w_out, conv_w, conv_b, pool_w, pool_scale, g_q, g_k, g_mq, g_mk, g_mem,
           w_mem_kv):
    depth = w_in.shape[0]
    bp, seq, _ = x_prompt.shape
    bs, ts, _ = x_sample.shape
    n_pages = page_table.shape[1]
    past_len = n_pages * PAGE_SIZE
    bd = _same_head_matrix()
    tile4 = lambda g: jnp.tile(g, (1, N_HEADS))[:, None, :]

    w_in_bf = w_in.astype(BF16)
    w_out_bf = w_out.astype(BF16)
    g_q_t, g_k_t, g_mq_t, g_mk_t = tile4(g_q), tile4(g_k), tile4(g_mq), tile4(g_mk)
    layer_w = []
    for l in range(depth):
        layer_w.append(dict(
            g_norm=g_norm[l][None, :], w_in=w_in_bf[l], conv_w=conv_w[l], conv_b=conv_b[l][None, :],
            pool_w=_pool_block_diag(pool_w[l]).astype(BF16), pool_scale=pool_scale[l][None, :],
            g_q=g_q_t[l], g_k=g_k_t[l], g_mq=g_mq_t[l], bd=bd))

    mkt_all, mvt_all, mkt_bf, mv_bf = _memkv(mem_prompt, g_mem[:, None, :], w_mem_kv.astype(BF16), g_mk_t, bd)

    cache_kt = _tokens_minor(cache_k)
    cache_vt = _tokens_minor(cache_v)
    memkt_s = _tokens_minor(cache_mem_k)
    memvt_s = _tokens_minor(cache_mem_v)

    xp = x_prompt
    xs = x_sample.reshape(bs * ts, D_MODEL)
    kp_l, vp_l, ks_l, vs_l, cp_l, cs_l, pp_l, ps_l = ([] for _ in range(8))
    for l in range(depth):
        lw = layer_w[l]
        (q, kt, vt, kaug, vaug, kmean, yconv, ypool, ymem, az, cst, pst) = _prompt_front(xp, lw, mkt_bf[l], mv_bf[l])
        xp = _prompt_moba(q, kaug, vaug, kmean.reshape(bp, seq // ROW_TILE, W_GROUP), az,
                          xp, yconv, ypool, ymem, w_out_bf[l])
        kp_l.append(_tokens_major(kt))
        vp_l.append(_tokens_major(vt))
        cp_l.append(cst)
        pp_l.append(pst)
        (qs, ksn, vsn, mqs, azs, mzs, yconv_s, ypool_s, cst_s, pst_s) = _sample_front(
            xs, lw, state_conv[l], state_pool[l], past_len)
        yatt_s, ymem_s = _sample_moba(l, page_table, qs, ksn, vsn, mqs, azs, mzs, memkt_s, memvt_s,
                                      cache_kt, cache_vt)
        xs = _out_proj(xs, yconv_s, ypool_s, yatt_s, ymem_s, w_out_bf[l])
        ks_l.append(ksn.reshape(bs, ts, N_HEADS, HEAD_DIM))
        vs_l.append(vsn.reshape(bs, ts, N_HEADS, HEAD_DIM))
        cs_l.append(cst_s)
        ps_l.append(pst_s)

    return (xp, xs.reshape(bs, ts, D_MODEL), jnp.stack(kp_l), jnp.stack(vp_l), jnp.stack(ks_l), jnp.stack(vs_l),
            jnp.stack(cp_l), jnp.stack(cs_l), jnp.stack(pp_l), jnp.stack(ps_l),
            _tokens_major(mkt_all), _tokens_major(mvt_all))
```

```python
import functools
import struct

import jax
import jax.numpy as jnp
from jax import lax
from jax.experimental import pallas as pl
from jax.experimental.pallas import tpu as pltpu

F32 = jnp.float32
BF16 = jnp.bfloat16

D_MODEL = 1024
W_GROUP = 256
N_HEADS = 4
HEAD_DIM = 64
N_MEM = 256
CONV_W = 3
POOL_WINDOWS = (2, 4, 8, 16)
POOL_STATE = 15
MOBA_BLOCK = 256
MOBA_TOPK = 3
PAGE_SIZE = 128
N_IN_SPLITS = 12
EPS = 1e-6
NEG_INF = float("-inf")

ROW_TILE = 256
FRONT_TILES = 4
MOBA_TILES = 2
POOL_HALO = 32
CONV_HALO = 8
PAGES_PER_STEP = 16
CHUNK_BUFS = 4
PAGES_PER_BLOCK = MOBA_BLOCK // PAGE_SIZE
VMEM_LIMIT = 56 * 1024 * 1024

SPLIT_ORDER = (10, 7, 6, 8, 11, 9, 4, 5, 2, 0, 1, 3)
SPLIT_COLUMN = {g: c for c, g in enumerate(SPLIT_ORDER)}

MAX_BLOCKS = 32
AUG_LANES = 64
K_GROUP_LANES = HEAD_DIM + AUG_LANES
K_AUG_LANES = N_HEADS * K_GROUP_LANES
V_AUG_ROWS = 80
MASKED = -1e30


def _bf16_round(x):
    bits = struct.unpack("<I", struct.pack("<f", x))[0]
    bits = (bits + 0x7FFF + ((bits >> 16) & 1)) & 0xFFFF0000
    return struct.unpack("<f", struct.pack("<I", bits))[0]


LOG2E = 1.4426950408889634
LOG2E_HI = _bf16_round(LOG2E)
LOG2E_LO = LOG2E - LOG2E_HI


def _slope(h):
    return 2.0 ** (-8.0 * (h + 1) / N_HEADS)


def _nt(a, b):
    return lax.dot_general(a, b, (((1,), (1,)), ((), ())), preferred_element_type=F32)


def _split_bf16(x):
    hi = x.astype(BF16)
    lo = (x - hi.astype(F32)).astype(BF16)
    return hi, lo


def _dot3(a, b, dot):
    ah, al = _split_bf16(a)
    bh, bl = _split_bf16(b)
    return dot(ah, bh) + dot(ah, bl) + dot(al, bh)


def _nn(a, b):
    return jnp.dot(a, b, preferred_element_type=F32)


def _rms(x, g):
    ms = jnp.mean(x * x, axis=-1, keepdims=True)
    return x * lax.rsqrt(ms + EPS) * g


def _head_rms(x, g, bd):
    hi, lo = _split_bf16(x * x)
    ssq = _nn(hi, bd) + _nn(lo, bd)
    return x * lax.rsqrt(ssq * (1.0 / HEAD_DIM) + EPS) * g


def _silu(z):
    return z / (1.0 + jnp.exp(-z))


def _top3_mask(gate, blk_iota, limit, axis):
    sel = jnp.zeros(gate.shape, F32)
    g = gate
    for _ in range(MOBA_TOPK):
        m = jnp.max(g, axis=axis, keepdims=True)
        idx = jnp.min(jnp.where(g == m, blk_iota, 1e9), axis=axis, keepdims=True)
        pick = blk_iota == idx
        sel = jnp.where(pick & (idx < limit), 1.0, sel)
        g = jnp.where(pick, NEG_INF, g)
    return sel


def _pool_lane_consts():
    lane = lax.broadcasted_iota(jnp.int32, (1, W_GROUP), 1)
    w = jnp.where(lane < 64, 2, jnp.where(lane < 128, 4, jnp.where(lane < 192, 8, 16)))
    return lane, w


def _value_groups(vt, n_tok):
    ones_row = (lax.broadcasted_iota(jnp.int32, (V_AUG_ROWS - HEAD_DIM, n_tok), 0) == 0).astype(F32)
    parts = []
    for hd in range(N_HEADS):
        parts += [vt[hd * HEAD_DIM:(hd + 1) * HEAD_DIM, :], ones_row]
    return jnp.concatenate(parts, axis=0).astype(BF16)


def _memkv_kernel(mem_ref, g_ref, w_ref, gmk_ref, bd_ref, mkt_ref, mvt_ref, mkb_ref, mvaug_ref):
    h = _rms(mem_ref[...], g_ref[...]).astype(BF16)
    kv = _nn(h, w_ref[...])
    mk = _head_rms(kv[:, :W_GROUP], gmk_ref[...], bd_ref[...])
    mvt = kv[:, W_GROUP:].T
    mkt_ref[...] = mk.T
    mvt_ref[...] = mvt
    mkb_ref[...] = mk.astype(BF16)
    mvaug_ref[...] = _value_groups(mvt, N_MEM)


def _memkv(mem_prompt, g_mem, w_mem_kv_bf, g_mk_t, bd):
    depth = g_mem.shape[0]
    batch = mem_prompt.shape[0]
    out = lambda dt: jax.ShapeDtypeStruct((depth, batch, N_MEM, W_GROUP), dt)
    ospec = pl.BlockSpec((None, None, N_MEM, W_GROUP), lambda l, b: (l, b, 0, 0))
    vaug_rows = N_HEADS * V_AUG_ROWS
    return pl.pallas_call(
        _memkv_kernel,
        grid=(depth, batch),
        in_specs=[
            pl.BlockSpec((None, N_MEM, D_MODEL), lambda l, b: (b, 0, 0)),
            pl.BlockSpec((None, 1, D_MODEL), lambda l, b: (l, 0, 0)),
            pl.BlockSpec((None, D_MODEL, 2 * W_GROUP), lambda l, b: (l, 0, 0)),
            pl.BlockSpec((None, 1, W_GROUP), lambda l, b: (l, 0, 0)),
            pl.BlockSpec((W_GROUP, W_GROUP), lambda l, b: (0, 0)),
        ],
        out_specs=[ospec, ospec, ospec,
                   pl.BlockSpec((None, None, vaug_rows, N_MEM), lambda l, b: (l, b, 0, 0))],
        out_shape=[out(F32), out(F32), out(BF16),
                   jax.ShapeDtypeStruct((depth, batch, vaug_rows, N_MEM), BF16)],
        compiler_params=pltpu.CompilerParams(dimension_semantics=("arbitrary", "arbitrary")),
        name="memkv",
    )(mem_prompt, g_mem, w_mem_kv_bf, g_mk_t, bd)


def _prompt_front_kernel(x_ref, gn_ref, win_ref, cw_ref, cb_ref, pw_ref, ps_ref, gq_ref, gk_ref, gmq_ref,
                         bd_ref, memk_ref, memvaug_ref,
                         q_ref, kt_ref, vt_ref, kaug_ref, vaug_ref, kmean_ref,
                         yconv_ref, ypool_ref, ymem_ref, az_ref, cstate_ref, pstate_ref,
                         u_scr, e_scr, s2_scr, s4_scr, s8_scr, proj_scr, ot_scr):
    step = pl.program_id(1)
    T = ROW_TILE
    H = POOL_HALO
    NT = FRONT_TILES

    @pl.when(step == 0)
    def _():
        u_scr[0:CONV_HALO, :] = jnp.zeros((CONV_HALO, W_GROUP), F32)
        e_scr[0:H, :] = jnp.zeros((H, W_GROUP), F32)

    bd = bd_ref[...]
    cw = cw_ref[...]
    lane, wlane = _pool_lane_consts()
    lane64 = lax.broadcasted_iota(jnp.int32, (T, AUG_LANES), 1)
    key_idx = lax.broadcasted_iota(jnp.int32, (T, AUG_LANES), 0).astype(F32)
    slope_lanes = (lane64 == MAX_BLOCKS) | (lane64 == MAX_BLOCKS + 1)

    for sub in range(NT):
        t = step * NT + sub
        rows = slice(sub * T, (sub + 1) * T)

        h = _rms(x_ref[rows, :], gn_ref[...]).astype(BF16)
        proj_scr[sub] = _nn(h, win_ref[...])

        def proj(g, sub=sub):
            c = SPLIT_COLUMN[g]
            return proj_scr[sub, :, c * W_GROUP:(c + 1) * W_GROUP]

        ub = CONV_HALO + sub * T
        u = proj(2) * proj(0)
        u_scr[ub:ub + T, :] = u
        conv = (cb_ref[...] + cw[0:1] * u_scr[ub - 2:ub - 2 + T, :]
                + cw[1:2] * u_scr[ub - 1:ub - 1 + T, :] + cw[2:3] * u)
        yconv_ref[rows, :] = (proj(1) * conv * _silu(proj(3))).astype(BF16)

        eb = sub * T
        pv = proj(4)
        e_scr[eb + H:eb + H + T, :] = pv
        s2_scr[sub, 8:H + T, :] = e_scr[eb + 8:eb + H + T, :] + e_scr[eb + 7:eb + H + T - 1, :]
        s4_scr[sub, 16:H + T, :] = s2_scr[sub, 16:H + T, :] + s2_scr[sub, 14:H + T - 2, :]
        s8_scr[sub, 24:H + T, :] = s4_scr[sub, 24:H + T, :] + s4_scr[sub, 20:H + T - 4, :]
        s16 = s8_scr[sub, H:H + T, :] + s8_scr[sub, H - 8:H + T - 8, :]
        wsum = jnp.where(lane < 64, s2_scr[sub, H:H + T, :],
                         jnp.where(lane < 128, s4_scr[sub, H:H + T, :],
                                   jnp.where(lane < 192, s8_scr[sub, H:H + T, :], s16)))
        pos = t * T + lax.broadcasted_iota(jnp.int32, (T, 1), 0)
        cnt = jnp.minimum(pos + 1, wlane).astype(F32)
        pooled = (wsum / cnt - pv).astype(BF16)
        pool_out = _nn(pooled, pw_ref[...]) * ps_ref[...]
        ypool_ref[rows, :] = (pool_out * _silu(proj(5))).astype(BF16)

        q_ref[rows, :] = _head_rms(proj(6), gq_ref[...], bd) * (HEAD_DIM ** -0.5)
        k = _head_rms(proj(7), gk_ref[...], bd)
        v = proj(8)
        vt = v.T
        kt_ref[:, rows] = k.T
        vt_ref[:, rows] = vt
        kmean_ref[sub] = jnp.mean(k, axis=0, keepdims=True)
        onehot = (lane64 == t).astype(F32)
        kparts = []
        for hd in range(N_HEADS):
            kparts += [k[:, hd * HEAD_DIM:(hd + 1) * HEAD_DIM],
                       jnp.where(slope_lanes, _slope(hd) * key_idx, onehot)]
        kaug_ref[sub] = jnp.concatenate(kparts, axis=1).astype(BF16)
        vaug_ref[sub] = _value_groups(vt, T)
        az_ref[rows, :] = _silu(proj(9)).astype(BF16)

        mqt = (_head_rms(proj(10), gmq_ref[...], bd) * (HEAD_DIM ** -0.5 * LOG2E)).T.astype(BF16)
        for hd in range(N_HEADS):
            hs = slice(hd * HEAD_DIM, (hd + 1) * HEAD_DIM)
            s = _nn(memk_ref[:, hs], mqt[hs, :])
            p = jnp.exp2(s - jnp.max(s, axis=0, keepdims=True))
            acc = _nn(memvaug_ref[hd * V_AUG_ROWS:(hd + 1) * V_AUG_ROWS, :], p.astype(BF16))
            ot_scr[sub, hs, :] = acc[0:HEAD_DIM] / acc[HEAD_DIM:HEAD_DIM + 1]
        ymem_ref[rows, :] = (ot_scr[sub].T * _silu(proj(11))).astype(BF16)

    last = NT * T
    cstate_ref[...] = u_scr[CONV_HALO + last - 2:CONV_HALO + last, :]
    pstate_ref[...] = e_scr[H + last - POOL_STATE:H + last, :]
    u_scr[0:CONV_HALO, :] = u_scr[last:last + CONV_HALO, :]
    e_scr[0:H, :] = e_scr[last:last + H, :]


def _prompt_front(x, lw, memk_bf, memvaug):
    batch, seq, _ = x.shape
    nt = seq // ROW_TILE
    NT = FRONT_TILES
    R = NT * ROW_TILE
    assert nt <= MAX_BLOCKS and nt % NT == 0
    row = lambda b, t: (b, t, 0)
    const2 = lambda b, t: (0, 0)
    tile = lambda dt: jax.ShapeDtypeStruct((batch, seq, W_GROUP), dt)
    tile_t = jax.ShapeDtypeStruct((batch, W_GROUP, seq), F32)
    tile_spec = pl.BlockSpec((None, R, W_GROUP), row)
    tile_t_spec = pl.BlockSpec((None, W_GROUP, R), lambda b, t: (b, 0, t))
    kaug_spec = pl.BlockSpec((None, NT, ROW_TILE, K_AUG_LANES), lambda b, t: (b, t, 0, 0))
    vaug_spec = pl.BlockSpec((None, NT, N_HEADS * V_AUG_ROWS, ROW_TILE), lambda b, t: (b, t, 0, 0))
    vec = pl.BlockSpec((1, W_GROUP), const2)
    return pl.pallas_call(
        _prompt_front_kernel,
        grid=(batch, nt // NT),
        in_specs=[
            pl.BlockSpec((None, R, D_MODEL), row),
            pl.BlockSpec((1, D_MODEL), const2),
            pl.BlockSpec((D_MODEL, N_IN_SPLITS * W_GROUP), const2),
            pl.BlockSpec((CONV_W, W_GROUP), const2),
            vec,
            pl.BlockSpec((W_GROUP, W_GROUP), const2),
            vec, vec, vec, vec,
            pl.BlockSpec((W_GROUP, W_GROUP), const2),
            pl.BlockSpec((None, N_MEM, W_GROUP), lambda b, t: (b, 0, 0)),
            pl.BlockSpec((None, N_HEADS * V_AUG_ROWS, N_MEM), lambda b, t: (b, 0, 0)),
        ],
        out_specs=[
            tile_spec, tile_t_spec, tile_t_spec, kaug_spec, vaug_spec,
            pl.BlockSpec((None, NT, 1, W_GROUP), lambda b, t: (b, t, 0, 0)),
            tile_spec, tile_spec, tile_spec, tile_spec,
            pl.BlockSpec((None, CONV_W - 1, W_GROUP), lambda b, t: (b, 0, 0)),
            pl.BlockSpec((None, POOL_STATE, W_GROUP), lambda b, t: (b, 0, 0)),
        ],
        out_shape=[
            tile(F32), tile_t, tile_t,
            jax.ShapeDtypeStruct((batch, nt, ROW_TILE, K_AUG_LANES), BF16),
            jax.ShapeDtypeStruct((batch, nt, N_HEADS * V_AUG_ROWS, ROW_TILE), BF16),
            jax.ShapeDtypeStruct((batch, nt, 1, W_GROUP), F32),
            tile(BF16), tile(BF16), tile(BF16), tile(BF16),
            jax.ShapeDtypeStruct((batch, CONV_W - 1, W_GROUP), F32),
            jax.ShapeDtypeStruct((batch, POOL_STATE, W_GROUP), F32),
        ],
        scratch_shapes=[
            pltpu.VMEM((CONV_HALO + R, W_GROUP), F32),
            pltpu.VMEM((POOL_HALO + R, W_GROUP), F32),
            pltpu.VMEM((NT, POOL_HALO + ROW_TILE, W_GROUP), F32),
            pltpu.VMEM((NT, POOL_HALO + ROW_TILE, W_GROUP), F32),
            pltpu.VMEM((NT, POOL_HALO + ROW_TILE, W_GROUP), F32),
            pltpu.VMEM((NT, ROW_TILE, N_IN_SPLITS * W_GROUP), F32),
            pltpu.VMEM((NT, W_GROUP, ROW_TILE), F32),
        ],
        compiler_params=pltpu.CompilerParams(dimension_semantics=("arbitrary", "arbitrary"),
                                             vmem_limit_bytes=VMEM_LIMIT),
        name="prompt_front",
    )(x, lw["g_norm"], lw["w_in"], lw["conv_w"], lw["conv_b"], lw["pool_w"], lw["pool_scale"],
      lw["g_q"], lw["g_k"], lw["g_mq"], lw["bd"], memk_bf, memvaug)


def _prompt_moba_kernel(q_ref, kaug_ref, vaug_ref, kmean_ref, az_ref, x_ref, yconv_ref, ypool_ref, ymem_ref, wout_ref,
                        y_ref, qaug_scr, acc_scr, outt_scr, s_scr):
    step = pl.program_id(1)
    T = ROW_TILE
    MT = MOBA_TILES
    nb = kmean_ref.shape[0]
    km = kmean_ref[...]
    blk_iota = lax.broadcasted_iota(jnp.int32, (nb, T), 0)
    blk_f = blk_iota.astype(F32)
    row8 = lax.broadcasted_iota(jnp.int32, (8, T), 0)
    log2e_rows = jnp.where(row8 == 0, LOG2E_HI, jnp.where(row8 == 1, LOG2E_LO, 0.0))
    pad_rows = jnp.zeros((K_GROUP_LANES - HEAD_DIM - MAX_BLOCKS - 8, T), F32)
    causal = lax.broadcasted_iota(jnp.int32, (T, T), 0) <= lax.broadcasted_iota(jnp.int32, (T, T), 1)
    chains = [(tile, hd) for tile in range(MT) for hd in range(N_HEADS)]

    def tile_block(tile):
        return step * MT + tile

    def kgroup(n, hd):
        return kaug_ref[n, :, hd * K_GROUP_LANES:(hd + 1) * K_GROUP_LANES]

    def vgroup(n, hd):
        return vaug_ref[n, hd * V_AUG_ROWS:(hd + 1) * V_AUG_ROWS, :]

    m0, mx0 = [], []
    for tile in range(MT):
        i = tile_block(tile)
        i_f = i.astype(F32)
        qt = q_ref[tile * T:(tile + 1) * T, :].T
        for hd in range(N_HEADS):
            c = tile * N_HEADS + hd
            hs = slice(hd * HEAD_DIM, (hd + 1) * HEAD_DIM)
            qt_h = qt[hs, :]
            gate = jnp.where(blk_f < i_f, _dot3(km[:, hs], qt_h, _nn), NEG_INF)
            sel = _top3_mask(gate, blk_f, i_f, axis=0)
            bias = jnp.where((sel > 0.0) | (blk_iota == i), 0.0, MASKED)
            if nb < MAX_BLOCKS:
                bias = jnp.concatenate([bias, jnp.zeros((MAX_BLOCKS - nb, T), F32)], axis=0)
            qaug_scr[c] = jnp.concatenate([qt_h * LOG2E, bias, log2e_rows, pad_rows], axis=0).astype(BF16)
            s = jnp.where(causal, _nn(kgroup(i, hd), qaug_scr[c]), NEG_INF)
            m = jnp.max(s, axis=0, keepdims=True)
            acc_scr[c] = _nn(vgroup(i, hd), jnp.exp2(s - m).astype(BF16))
            m0.append(m)
            s = _nn(kgroup(0, hd), qaug_scr[c])
            s_scr[0, c] = s
            mx0.append(jnp.max(s, axis=0, keepdims=True))

    def stage(n, cur, nxt, ms, mxs):
        ms_out, mxs_out = [], []
        for c, (tile, hd) in enumerate(chains):
            i = tile_block(tile)
            s_next = _nn(kgroup(jnp.minimum(n + 1, i), hd), qaug_scr[c])
            s_scr[nxt, c] = s_next
            mxs_out.append(jnp.max(s_next, axis=0, keepdims=True))
            cn = jnp.where(n < i, (n - i).astype(F32) * (_slope(hd) * MOBA_BLOCK * LOG2E), NEG_INF)
            m_new = jnp.maximum(ms[c], mxs[c] + cn)
            alpha = jnp.exp2(ms[c] - m_new)
            p = jnp.exp2(s_scr[cur, c] - (m_new - cn))
            acc_scr[c] = alpha * acc_scr[c] + _nn(vgroup(jnp.minimum(n, i), hd), p.astype(BF16))
            ms_out.append(m_new)
        return tuple(ms_out), tuple(mxs_out)

    def body(j, carry):
        ms, mxs = stage(2 * j, 0, 1, *carry)
        return stage(2 * j + 1, 1, 0, ms, mxs)

    lax.fori_loop(0, (tile_block(MT - 1) + 1) // 2, body, (tuple(m0), tuple(mx0)))
    for tile in range(MT):
        rows = slice(tile * T, (tile + 1) * T)
        for hd in range(N_HEADS):
            acc = acc_scr[tile * N_HEADS + hd]
            outt_scr[tile, hd * HEAD_DIM:(hd + 1) * HEAD_DIM, :] = acc[0:HEAD_DIM] / acc[HEAD_DIM:HEAD_DIM + 1]
        yatt = (outt_scr[tile].T * az_ref[rows, :].astype(F32)).astype(BF16)
        y = x_ref[rows, :]
        for g, part in enumerate((yconv_ref[rows, :], ypool_ref[rows, :], yatt, ymem_ref[rows, :])):
            y = y + _nn(part, wout_ref[g * W_GROUP:(g + 1) * W_GROUP, :])
        y_ref[rows, :] = y


def _prompt_moba(q, kaug, vaug, kmean, az, x, yconv, ypool, ymem, w_out_bf):
    batch, seq, _ = q.shape
    nt = seq // ROW_TILE
    MT = MOBA_TILES
    assert nt % MT == 0
    R = MT * ROW_TILE
    row = lambda b, t: (b, t, 0)
    part = pl.BlockSpec((None, R, W_GROUP), row)
    return pl.pallas_call(
        _prompt_moba_kernel,
        grid=(batch, nt // MT),
        in_specs=[
            part,
            pl.BlockSpec((None, nt, ROW_TILE, K_AUG_LANES), lambda b, t: (b, 0, 0, 0)),
            pl.BlockSpec((None, nt, N_HEADS * V_AUG_ROWS, ROW_TILE), lambda b, t: (b, 0, 0, 0)),
            pl.BlockSpec((None, nt, W_GROUP), lambda b, t: (b, 0, 0)),
            part,
            pl.BlockSpec((None, R, D_MODEL), row),
            part, part, part,
            pl.BlockSpec((D_MODEL, D_MODEL), lambda b, t: (0, 0)),
        ],
        out_specs=pl.BlockSpec((None, R, D_MODEL), row),
        out_shape=jax.ShapeDtypeStruct((batch, seq, D_MODEL), F32),
        scratch_shapes=[
            pltpu.VMEM((MT * N_HEADS, K_GROUP_LANES, ROW_TILE), BF16),
            pltpu.VMEM((MT * N_HEADS, V_AUG_ROWS, ROW_TILE), F32),
            pltpu.VMEM((MT, W_GROUP, ROW_TILE), F32),
            pltpu.VMEM((2, MT * N_HEADS, MOBA_BLOCK, ROW_TILE), F32),
        ],
        compiler_params=pltpu.CompilerParams(dimension_semantics=("arbitrary", "arbitrary"),
                                             vmem_limit_bytes=VMEM_LIMIT),
        name="prompt_moba",
    )(q, kaug, vaug, kmean, az, x, yconv, ypool, ymem, w_out_bf)


def _out_kernel(x_ref, yconv_ref, ypool_ref, yatt_ref, ymem_ref, w_ref, y_ref):
    acc = x_ref[...]
    for g, ref in enumerate((yconv_ref, ypool_ref, yatt_ref, ymem_ref)):
        acc = acc + _nn(ref[...].astype(BF16), w_ref[g * W_GROUP:(g + 1) * W_GROUP, :])
    y_ref[...] = acc


def _out_proj(x2, yconv, ypool, yatt, ymem, w_out_bf):
    rows = x2.shape[0]
    row = lambda r: (r, 0)
    part = pl.BlockSpec((ROW_TILE, W_GROUP), row)
    return pl.pallas_call(
        _out_kernel,
        grid=(rows // ROW_TILE,),
        in_specs=[pl.BlockSpec((ROW_TILE, D_MODEL), row), part, part, part, part,
                  pl.BlockSpec((D_MODEL, D_MODEL), lambda r: (0, 0))],
        out_specs=pl.BlockSpec((ROW_TILE, D_MODEL), row),
        out_shape=jax.ShapeDtypeStruct((rows, D_MODEL), F32),
        compiler_params=pltpu.CompilerParams(dimension_semantics=("arbitrary",),
                                             vmem_limit_bytes=VMEM_LIMIT),
        name="out_proj",
    )(x2, yconv, ypool, yatt, ymem, w_out_bf)


def _sample_front_kernel(start, x_ref, gn_ref, win_ref, cw_ref, cb_ref, pw_ref, ps_ref, gq_ref, gk_ref, gmq_ref,
                         bd_ref, sconv_ref, spool_ref,
                         q_ref, k_ref, v_ref, mq_ref, az_ref, mz_ref, yconv_ref, ypool_ref, cstate_ref, pstate_ref,
                         u_scr, e_scr):
    nb, ts = sconv_ref.shape[0], u_scr.shape[1] - CONV_HALO
    rows = nb * ts
    h = _rms(x_ref[...], gn_ref[...]).astype(BF16)
    bd = bd_ref[...]

    def proj(g):
        c = SPLIT_COLUMN[g]
        return _nn(h, win_ref[:, c * W_GROUP:(c + 1) * W_GROUP])

    def to3(a):
        return a.reshape(nb, ts, W_GROUP)

    def to2(a):
        return a.reshape(rows, W_GROUP)

    u = proj(2) * proj(0)
    u_scr[:, 0:CONV_HALO - 2, :] = jnp.zeros((nb, CONV_HALO - 2, W_GROUP), F32)
    u_scr[:, CONV_HALO - 2:CONV_HALO, :] = sconv_ref[...]
    u_scr[:, CONV_HALO:CONV_HALO + ts, :] = to3(u)
    cw = cw_ref[...]
    conv3 = (cw[0:1] * u_scr[:, CONV_HALO - 2:CONV_HALO - 2 + ts, :]
             + cw[1:2] * u_scr[:, CONV_HALO - 1:CONV_HALO - 1 + ts, :]
             + cw[2:3] * u_scr[:, CONV_HALO:CONV_HALO + ts, :])
    conv = cb_ref[...] + to2(conv3)
    yconv_ref[...] = proj(1) * conv * _silu(proj(3))
    cstate_ref[...] = u_scr[:, CONV_HALO + ts - 2:CONV_HALO + ts, :]

    H = 16
    pv = proj(4)
    e_scr[:, 0:1, :] = jnp.zeros((nb, 1, W_GROUP), F32)
    e_scr[:, 1:H, :] = spool_ref[...]
    e_scr[:, H:H + ts, :] = to3(pv)
    lane, wlane = _pool_lane_consts()
    wsum = jnp.zeros((nb, ts, W_GROUP), F32)
    for j in range(max(POOL_WINDOWS)):
        shifted = e_scr[:, H - j:H - j + ts, :]
        wsum = wsum + (shifted if j < min(POOL_WINDOWS) else jnp.where(wlane > j, shifted, 0.0))
    pos = start + lax.broadcasted_iota(jnp.int32, (ts, 1), 0)
    cnt = jnp.minimum(pos + 1, wlane).astype(F32)
    pooled = (to2(wsum / cnt) - pv).astype(BF16)
    pool_out = _nn(pooled, pw_ref[...]) * ps_ref[...]
    ypool_ref[...] = pool_out * _silu(proj(5))
    pstate_ref[...] = e_scr[:, H + ts - POOL_STATE:H + ts, :]

    q_ref[...] = _head_rms(proj(6), gq_ref[...], bd) * (HEAD_DIM ** -0.5)
    k_ref[...] = _head_rms(proj(7), gk_ref[...], bd)
    v_ref[...] = proj(8)
    az_ref[...] = _silu(proj(9))
    mq_ref[...] = _head_rms(proj(10), gmq_ref[...], bd) * (HEAD_DIM ** -0.5)
    mz_ref[...] = _silu(proj(11))


def _sample_front(x2, lw, sconv, spool, start):
    nb = sconv.shape[0]
    rows = x2.shape[0]
    ts = rows // nb
    full = lambda shape: pl.BlockSpec(shape, lambda i, n=len(shape): (0,) * n)
    t2 = jax.ShapeDtypeStruct((rows, W_GROUP), F32)
    return pl.pallas_call(
        functools.partial(_sample_front_kernel, start),
        grid=(1,),
        in_specs=[
            full((rows, D_MODEL)), full((1, D_MODEL)), full((D_MODEL, N_IN_SPLITS * W_GROUP)),
            full((CONV_W, W_GROUP)), full((1, W_GROUP)), full((W_GROUP, W_GROUP)),
            full((1, W_GROUP)), full((1, W_GROUP)), full((1, W_GROUP)), full((1, W_GROUP)),
            full((W_GROUP, W_GROUP)),
            full((nb, CONV_W - 1, W_GROUP)), full((nb, POOL_STATE, W_GROUP)),
        ],
        out_specs=[full((rows, W_GROUP))] * 8 + [full((nb, CONV_W - 1, W_GROUP)), full((nb, POOL_STATE, W_GROUP))],
        out_shape=[t2] * 8 + [jax.ShapeDtypeStruct((nb, CONV_W - 1, W_GROUP), F32),
                              jax.ShapeDtypeStruct((nb, POOL_STATE, W_GROUP), F32)],
        scratch_shapes=[
            pltpu.VMEM((nb, CONV_HALO + ts, W_GROUP), F32),
            pltpu.VMEM((nb, 16 + ts, W_GROUP), F32),
        ],
        compiler_params=pltpu.CompilerParams(dimension_semantics=("arbitrary",),
                                             vmem_limit_bytes=VMEM_LIMIT),
        name="sample_front",
    )(x2, lw["g_norm"], lw["w_in"], lw["conv_w"], lw["conv_b"], lw["pool_w"], lw["pool_scale"],
      lw["g_q"], lw["g_k"], lw["g_mq"], lw["bd"], sconv, spool)


def _head_rows(a, ts):
    r = lax.broadcasted_iota(jnp.int32, (N_HEADS * ts, W_GROUP), 0) // ts
    c = lax.broadcasted_iota(jnp.int32, (N_HEADS * ts, W_GROUP), 1) // HEAD_DIM
    return jnp.where(r == c, jnp.concatenate([a] * N_HEADS, axis=0), 0.0)


def _head_diag(o, ts):
    r = lax.broadcasted_iota(jnp.int32, (N_HEADS * ts, W_GROUP), 0) // ts
    c = lax.broadcasted_iota(jnp.int32, (N_HEADS * ts, W_GROUP), 1) // HEAD_DIM
    o = jnp.where(r == c, o, 0.0)
    out = o[0:ts]
    for hd in range(1, N_HEADS):
        out = out + o[hd * ts:(hd + 1) * ts]
    return out


def _sample_moba_kernel(layer, past_len, pt_ref, q_ref, knew_ref, vnew_ref, mq_ref, az_ref, mz_ref, memkt_ref,
                        memvt_ref, ck_hbm, cv_hbm, yatt_ref, ymem_ref, buf, sem, s_scr, p_scr):
    P = PAGES_PER_STEP
    BPS = P // PAGES_PER_BLOCK
    b = pl.program_id(0)
    n_seqs = pl.num_programs(0)
    ts = q_ref.shape[0]
    R = N_HEADS * ts
    n_pages = past_len // PAGE_SIZE
    nblk = past_len // MOBA_BLOCK
    nc = n_pages // P
    n_chunks = 2 * nc

    def chunk_copies(seq, g):
        src = ck_hbm if g < nc else cv_hbm
        first = (g % nc) * P
        slot = g % CHUNK_BUFS
        return [pltpu.make_async_copy(src.at[layer, pt_ref[seq, first + j]], buf.at[slot, j], sem.at[slot])
                for j in range(P)]

    def start(seq, g):
        for j, cp in enumerate(chunk_copies(seq, g)):
            cp.start(priority=j % 2)

    def wait(seq, g):
        for cp in chunk_copies(seq, g):
            cp.wait()

    def prefetch_after(g):
        ahead = g + CHUNK_BUFS - 1
        if ahead < n_chunks:
            start(b, ahead)
        else:
            @pl.when(b + 1 < n_seqs)
            def _():
                start(b + 1, ahead - n_chunks)

    @pl.when(b == 0)
    def _():
        for g in range(CHUNK_BUFS - 1):
            start(b, g)

    qrows = _head_rows(q_ref[...], ts)
    row_t = (lax.broadcasted_iota(jnp.int32, (R, 1), 0) % ts).astype(F32)
    row_h = lax.broadcasted_iota(jnp.int32, (R, 1), 0) // ts
    slope = jnp.where(row_h == 0, _slope(0), jnp.where(row_h == 1, _slope(1),
                                                        jnp.where(row_h == 2, _slope(2), _slope(3))))

    qb = qrows.astype(BF16)
    lane8 = lax.broadcasted_iota(jnp.int32, (W_GROUP, BPS), 1)
    ksums = []
    for g in range(nc):
        prefetch_after(g)
        wait(b, g)
        sums = jnp.zeros((W_GROUP, BPS), F32)
        for j in range(0, P, PAGES_PER_BLOCK):
            ka, kb = buf[g % CHUNK_BUFS, j], buf[g % CHUNK_BUFS, j + 1]
            s_scr[g * P + j] = _nn(qb, ka.astype(BF16))
            s_scr[g * P + j + 1] = _nn(qb, kb.astype(BF16))
            tot = jnp.sum(ka + kb, axis=1, keepdims=True)
            sums = jnp.where(lane8 == j // PAGES_PER_BLOCK, tot, sums)
        ksums.append(sums)

    gate = jnp.concatenate([_dot3(qrows, ks * (1.0 / MOBA_BLOCK), _nn) for ks in ksums], axis=1)
    blk_iota = lax.broadcasted_iota(jnp.int32, (R, nblk), 1).astype(F32)
    sel = _top3_mask(gate, blk_iota, float(nblk), axis=-1)
    blk_term = jnp.where(sel > 0.0, slope * (blk_iota * MOBA_BLOCK - past_len), NEG_INF)
    col = lax.broadcasted_iota(jnp.int32, (R, PAGE_SIZE), 1).astype(F32)
    page_term = [slope * (col + float(k * PAGE_SIZE)) for k in range(PAGES_PER_BLOCK)]
    own = []
    m = jnp.full((R, 1), NEG_INF, F32)
    for j in range(ts):
        sj = jnp.sum(qrows * knew_ref[j:j + 1, :], axis=-1, keepdims=True) + slope * float(j)
        sj = jnp.where(row_t >= float(j), sj, NEG_INF)
        own.append(sj)
        m = jnp.maximum(m, sj)
    mvec = jnp.full((R, PAGE_SIZE), NEG_INF, F32)
    for n in range(nblk):
        for k in range(PAGES_PER_BLOCK):
            pg = n * PAGES_PER_BLOCK + k
            s = s_scr[pg] + page_term[k] + blk_term[:, n:n + 1]
            s_scr[pg] = s
            mvec = jnp.maximum(mvec, s)
    m = jnp.maximum(m, jnp.max(mvec, axis=-1, keepdims=True))
    p_own = [jnp.exp(sj - m) for sj in own]
    lvec = jnp.zeros((R, PAGE_SIZE), F32)
    for pg in range(n_pages):
        p = jnp.exp(s_scr[pg] - m)
        p_scr[pg] = p.astype(BF16)
        lvec = lvec + p
    l = sum(p_own) + jnp.sum(lvec, axis=-1, keepdims=True)

    acc = jnp.zeros((R, W_GROUP), F32)
    for g in range(nc, n_chunks):
        prefetch_after(g)
        wait(b, g)
        for j in range(P):
            acc = acc + _nt(p_scr[(g - nc) * P + j], buf[g % CHUNK_BUFS, j].astype(BF16))
    for j in range(ts):
        acc = acc + p_own[j] * vnew_ref[j:j + 1, :]
    yatt_ref[...] = _head_diag(acc / l, ts) * az_ref[...]

    mrows = _head_rows(mq_ref[...], ts).astype(BF16)
    s = _nn(mrows, memkt_ref[...].astype(BF16))
    p = jnp.exp(s - jnp.max(s, axis=-1, keepdims=True))
    lm = jnp.sum(p, axis=-1, keepdims=True)
    o = _nt(p.astype(BF16), memvt_ref[...].astype(BF16)) / lm
    ymem_ref[...] = _head_diag(o, ts) * mz_ref[...]


def _sample_moba(layer, page_table, q, knew, vnew, mq, az, mz, memkt, memvt, cache_kt, cache_vt):
    nb, n_pages = page_table.shape
    rows = q.shape[0]
    ts = rows // nb
    P = PAGES_PER_STEP
    assert n_pages % P == 0 and P % PAGES_PER_BLOCK == 0
    assert (2 * n_pages // P) % CHUNK_BUFS == 0
    past_len = n_pages * PAGE_SIZE
    R = N_HEADS * ts
    seq = pl.BlockSpec((ts, W_GROUP), lambda b, pt: (b, 0))
    mem = pl.BlockSpec((None, None, W_GROUP, N_MEM), lambda b, pt: (layer, b, 0, 0))
    hbm = pl.BlockSpec(memory_space=pl.ANY)
    out = jax.ShapeDtypeStruct((rows, W_GROUP), F32)
    grid_spec = pltpu.PrefetchScalarGridSpec(
        num_scalar_prefetch=1,
        grid=(nb,),
        in_specs=[seq, seq, seq, seq, seq, seq, mem, mem, hbm, hbm],
        out_specs=[seq, seq],
        scratch_shapes=[
            pltpu.VMEM((CHUNK_BUFS, P, W_GROUP, PAGE_SIZE), F32),
            pltpu.SemaphoreType.DMA((CHUNK_BUFS,)),
            pltpu.VMEM((n_pages, R, PAGE_SIZE), F32),
            pltpu.VMEM((n_pages, R, PAGE_SIZE), BF16),
        ],
    )
    return pl.pallas_call(
        functools.partial(_sample_moba_kernel, layer, past_len),
        grid_spec=grid_spec,
        out_shape=[out, out],
        compiler_params=pltpu.CompilerParams(dimension_semantics=("arbitrary",),
                                             vmem_limit_bytes=VMEM_LIMIT),
        name="sample_moba",
    )(page_table, q, knew, vnew, mq, az, mz, memkt, memvt, cache_kt, cache_vt)


def _same_head_matrix():
    r = lax.broadcasted_iota(jnp.int32, (W_GROUP, W_GROUP), 0) // HEAD_DIM
    c = lax.broadcasted_iota(jnp.int32, (W_GROUP, W_GROUP), 1) // HEAD_DIM
    return (r == c).astype(BF16)


def _pool_block_diag(pool_w_l):
    out = jnp.zeros((W_GROUP, W_GROUP), pool_w_l.dtype)
    for g in range(len(POOL_WINDOWS)):
        out = out.at[g * 64:(g + 1) * 64, g * 64:(g + 1) * 64].set(pool_w_l[g])
    return out


def _tokens_minor(a):
    lead = a.shape[:-3]
    n = len(lead)
    perm = tuple(range(n)) + (n + 1, n + 2, n)
    return a.transpose(perm).reshape(lead + (W_GROUP, a.shape[-3]))


def _tokens_major(a_t):
    lead = a_t.shape[:-2]
    n = len(lead)
    perm = tuple(range(n)) + (n + 2, n, n + 1)
    return a_t.reshape(lead + (N_HEADS, HEAD_DIM, a_t.shape[-1])).transpose(perm)


def kernel(x_prompt, x_sample, cache_k, cache_v, page_table, state_conv, state_pool, cache_mem_k, cache_mem_v,
           mem_prompt, g_norm, w_in, w_out, conv_w, conv_b, pool_w, pool_scale, g_q, g_k, g_mq, g_mk, g_mem,
           w_mem_kv):
    depth = w_in.shape[0]
    bp, seq, _ = x_prompt.shape
    bs, ts, _ = x_sample.shape
    n_pages = page_table.shape[1]
    past_len = n_pages * PAGE_SIZE
    bd = _same_head_matrix()
    tile4 = lambda g: jnp.tile(g, (1, N_HEADS))[:, None, :]

    w_in_bf = w_in.astype(BF16).reshape(depth, D_MODEL, N_IN_SPLITS, W_GROUP)[:, :, list(SPLIT_ORDER), :]
    w_in_bf = w_in_bf.reshape(depth, D_MODEL, N_IN_SPLITS * W_GROUP)
    w_out_bf = w_out.astype(BF16)
    g_q_t, g_k_t, g_mq_t, g_mk_t = tile4(g_q), tile4(g_k), tile4(g_mq), tile4(g_mk)
    layer_w = []
    for l in range(depth):
        layer_w.append(dict(
            g_norm=g_norm[l][None, :], w_in=w_in_bf[l], conv_w=conv_w[l], conv_b=conv_b[l][None, :],
            pool_w=_pool_block_diag(pool_w[l]).astype(BF16), pool_scale=pool_scale[l][None, :],
            g_q=g_q_t[l], g_k=g_k_t[l], g_mq=g_mq_t[l], bd=bd))

    mkt_all, mvt_all, mk_bf, mvaug = _memkv(mem_prompt, g_mem[:, None, :], w_mem_kv.astype(BF16), g_mk_t, bd)

    cache_kt = _tokens_minor(cache_k)
    cache_vt = _tokens_minor(cache_v)
    memkt_s = _tokens_minor(cache_mem_k)
    memvt_s = _tokens_minor(cache_mem_v)

    xp = x_prompt
    xs = x_sample.reshape(bs * ts, D_MODEL)
    kp_l, vp_l, ks_l, vs_l, cp_l, cs_l, pp_l, ps_l = ([] for _ in range(8))
    for l in range(depth):
        lw = layer_w[l]
        (q, kt, vt, kaug, vaug, kmean, yconv, ypool, ymem, az, cst, pst) = _prompt_front(xp, lw, mk_bf[l], mvaug[l])
        xp = _prompt_moba(q, kaug, vaug, kmean.reshape(bp, seq // ROW_TILE, W_GROUP), az,
                          xp, yconv, ypool, ymem, w_out_bf[l])
        kp_l.append(_tokens_major(kt))
        vp_l.append(_tokens_major(vt))
        cp_l.append(cst)
        pp_l.append(pst)
        (qs, ksn, vsn, mqs, azs, mzs, yconv_s, ypool_s, cst_s, pst_s) = _sample_front(
            xs, lw, state_conv[l], state_pool[l], past_len)
        yatt_s, ymem_s = _sample_moba(l, page_table, qs, ksn, vsn, mqs, azs, mzs, memkt_s, memvt_s,
                                      cache_kt, cache_vt)
        xs = _out_proj(xs, yconv_s, ypool_s, yatt_s, ymem_s, w_out_bf[l])
        ks_l.append(ksn.reshape(bs, ts, N_HEADS, HEAD_DIM))
        vs_l.append(vsn.reshape(bs, ts, N_HEADS, HEAD_DIM))
        cs_l.append(cst_s)
        ps_l.append(pst_s)

    return (xp, xs.reshape(bs, ts, D_MODEL), jnp.stack(kp_l), jnp.stack(vp_l), jnp.stack(ks_l), jnp.stack(vs_l),
            jnp.stack(cp_l), jnp.stack(cs_l), jnp.stack(pp_l), jnp.stack(ps_l),
            _tokens_major(mkt_all), _tokens_major(mvt_all))
```

```python
import functools
import struct

import jax
import jax.numpy as jnp
from jax import lax
from jax.experimental import pallas as pl
from jax.experimental.pallas import tpu as pltpu

F32 = jnp.float32
BF16 = jnp.bfloat16

D_MODEL = 1024
W_GROUP = 256
N_HEADS = 4
HEAD_DIM = 64
N_MEM = 256
CONV_W = 3
POOL_WINDOWS = (2, 4, 8, 16)
POOL_STATE = 15
MOBA_BLOCK = 256
MOBA_TOPK = 3
PAGE_SIZE = 128
N_IN_SPLITS = 12
EPS = 1e-6
NEG_INF = float("-inf")

ROW_TILE = 256
FRONT_TILES = 4
MOBA_TILES = 2
POOL_HALO = 32
CONV_HALO = 8
PAGES_PER_STEP = 16
CHUNK_BUFS = 4
PAGES_PER_BLOCK = MOBA_BLOCK // PAGE_SIZE
VMEM_LIMIT = 56 * 1024 * 1024

SPLIT_ORDER = (10, 7, 6, 8, 11, 9, 4, 5, 2, 0, 1, 3)
SPLIT_COLUMN = {g: c for c, g in enumerate(SPLIT_ORDER)}

MAX_BLOCKS = 32
AUG_LANES = 64
K_GROUP_LANES = HEAD_DIM + AUG_LANES
K_AUG_LANES = N_HEADS * K_GROUP_LANES
V_AUG_ROWS = 80
MASKED = -1e30


def _bf16_round(x):
    bits = struct.unpack("<I", struct.pack("<f", x))[0]
    bits = (bits + 0x7FFF + ((bits >> 16) & 1)) & 0xFFFF0000
    return struct.unpack("<f", struct.pack("<I", bits))[0]


LOG2E = 1.4426950408889634
LOG2E_HI = _bf16_round(LOG2E)
LOG2E_LO = LOG2E - LOG2E_HI


def _slope(h):
    return 2.0 ** (-8.0 * (h + 1) / N_HEADS)


def _nt(a, b):
    return lax.dot_general(a, b, (((1,), (1,)), ((), ())), preferred_element_type=F32)


def _split_bf16(x):
    hi = x.astype(BF16)
    lo = (x - hi.astype(F32)).astype(BF16)
    return hi, lo


def _dot3(a, b, dot):
    ah, al = _split_bf16(a)
    bh, bl = _split_bf16(b)
    return dot(ah, bh) + dot(ah, bl) + dot(al, bh)


def _nn(a, b):
    return jnp.dot(a, b, preferred_element_type=F32)


def _rms(x, g):
    ms = jnp.mean(x * x, axis=-1, keepdims=True)
    return x * lax.rsqrt(ms + EPS) * g


def _head_rms(x, g, bd):
    hi, lo = _split_bf16(x * x)
    ssq = _nn(hi, bd) + _nn(lo, bd)
    return x * lax.rsqrt(ssq * (1.0 / HEAD_DIM) + EPS) * g


def _silu(z):
    return z / (1.0 + jnp.exp(-z))


def _top3_mask(gate, blk_iota, limit, axis):
    sel = jnp.zeros(gate.shape, F32)
    g = gate
    for _ in range(MOBA_TOPK):
        m = jnp.max(g, axis=axis, keepdims=True)
        idx = jnp.min(jnp.where(g == m, blk_iota, 1e9), axis=axis, keepdims=True)
        pick = blk_iota == idx
        sel = jnp.where(pick & (idx < limit), 1.0, sel)
        g = jnp.where(pick, NEG_INF, g)
    return sel


def _pool_lane_consts():
    lane = lax.broadcasted_iota(jnp.int32, (1, W_GROUP), 1)
    w = jnp.where(lane < 64, 2, jnp.where(lane < 128, 4, jnp.where(lane < 192, 8, 16)))
    return lane, w


def _value_groups(vt, n_tok):
    ones_row = (lax.broadcasted_iota(jnp.int32, (V_AUG_ROWS - HEAD_DIM, n_tok), 0) == 0).astype(F32)
    parts = []
    for hd in range(N_HEADS):
        parts += [vt[hd * HEAD_DIM:(hd + 1) * HEAD_DIM, :], ones_row]
    return jnp.concatenate(parts, axis=0).astype(BF16)


def _memkv_kernel(mem_ref, g_ref, w_ref, gmk_ref, bd_ref, mkt_ref, mvt_ref, mkb_ref, mvaug_ref):
    h = _rms(mem_ref[...], g_ref[...]).astype(BF16)
    kv = _nn(h, w_ref[...])
    mk = _head_rms(kv[:, :W_GROUP], gmk_ref[...], bd_ref[...])
    mvt = kv[:, W_GROUP:].T
    mkt_ref[...] = mk.T
    mvt_ref[...] = mvt
    mkb_ref[...] = mk.astype(BF16)
    mvaug_ref[...] = _value_groups(mvt, N_MEM)


def _memkv(mem_prompt, g_mem, w_mem_kv_bf, g_mk_t, bd):
    depth = g_mem.shape[0]
    batch = mem_prompt.shape[0]
    out = lambda dt: jax.ShapeDtypeStruct((depth, batch, N_MEM, W_GROUP), dt)
    ospec = pl.BlockSpec((None, None, N_MEM, W_GROUP), lambda l, b: (l, b, 0, 0))
    vaug_rows = N_HEADS * V_AUG_ROWS
    return pl.pallas_call(
        _memkv_kernel,
        grid=(depth, batch),
        in_specs=[
            pl.BlockSpec((None, N_MEM, D_MODEL), lambda l, b: (b, 0, 0)),
            pl.BlockSpec((None, 1, D_MODEL), lambda l, b: (l, 0, 0)),
            pl.BlockSpec((None, D_MODEL, 2 * W_GROUP), lambda l, b: (l, 0, 0)),
            pl.BlockSpec((None, 1, W_GROUP), lambda l, b: (l, 0, 0)),
            pl.BlockSpec((W_GROUP, W_GROUP), lambda l, b: (0, 0)),
        ],
        out_specs=[ospec, ospec, ospec,
                   pl.BlockSpec((None, None, vaug_rows, N_MEM), lambda l, b: (l, b, 0, 0))],
        out_shape=[out(F32), out(F32), out(BF16),
                   jax.ShapeDtypeStruct((depth, batch, vaug_rows, N_MEM), BF16)],
        compiler_params=pltpu.CompilerParams(dimension_semantics=("arbitrary", "arbitrary")),
        name="memkv",
    )(mem_prompt, g_mem, w_mem_kv_bf, g_mk_t, bd)


def _prompt_front_kernel(n_prev, x_ref, gn_ref, win_ref, cw_ref, cb_ref, pw_ref, ps_ref, gq_ref, gk_ref, gmq_ref,
                         bd_ref, memk_ref, memvaug_ref, *refs):
    if n_prev:
        ktprev_ref, vtprev_ref, *refs = refs
    (q_ref, kt_ref, vt_ref, kaug_ref, vaug_ref, kmean_ref, yconv_ref, ypool_ref, ymem_ref, az_ref,
     cstate_ref, pstate_ref, u_scr, e_scr, s2_scr, s4_scr, s8_scr, proj_scr, ot_scr) = refs
    if n_prev:
        kt_ref[0:n_prev] = ktprev_ref[...]
        vt_ref[0:n_prev] = vtprev_ref[...]
    step = pl.program_id(1)
    T = ROW_TILE
    H = POOL_HALO
    NT = FRONT_TILES

    @pl.when(step == 0)
    def _():
        u_scr[0:CONV_HALO, :] = jnp.zeros((CONV_HALO, W_GROUP), F32)
        e_scr[0:H, :] = jnp.zeros((H, W_GROUP), F32)

    bd = bd_ref[...]
    cw = cw_ref[...]
    lane, wlane = _pool_lane_consts()
    lane64 = lax.broadcasted_iota(jnp.int32, (T, AUG_LANES), 1)
    key_idx = lax.broadcasted_iota(jnp.int32, (T, AUG_LANES), 0).astype(F32)
    slope_lanes = (lane64 == MAX_BLOCKS) | (lane64 == MAX_BLOCKS + 1)

    for sub in range(NT):
        t = step * NT + sub
        rows = slice(sub * T, (sub + 1) * T)

        h = _rms(x_ref[rows, :], gn_ref[...]).astype(BF16)
        proj_scr[sub] = _nn(h, win_ref[...])

        def proj(g, sub=sub):
            c = SPLIT_COLUMN[g]
            return proj_scr[sub, :, c * W_GROUP:(c + 1) * W_GROUP]

        ub = CONV_HALO + sub * T
        u = proj(2) * proj(0)
        u_scr[ub:ub + T, :] = u
        conv = (cb_ref[...] + cw[0:1] * u_scr[ub - 2:ub - 2 + T, :]
                + cw[1:2] * u_scr[ub - 1:ub - 1 + T, :] + cw[2:3] * u)
        yconv_ref[rows, :] = (proj(1) * conv * _silu(proj(3))).astype(BF16)

        eb = sub * T
        pv = proj(4)
        e_scr[eb + H:eb + H + T, :] = pv
        s2_scr[sub, 8:H + T, :] = e_scr[eb + 8:eb + H + T, :] + e_scr[eb + 7:eb + H + T - 1, :]
        s4_scr[sub, 16:H + T, :] = s2_scr[sub, 16:H + T, :] + s2_scr[sub, 14:H + T - 2, :]
        s8_scr[sub, 24:H + T, :] = s4_scr[sub, 24:H + T, :] + s4_scr[sub, 20:H + T - 4, :]
        s16 = s8_scr[sub, H:H + T, :] + s8_scr[sub, H - 8:H + T - 8, :]
        wsum = jnp.where(lane < 64, s2_scr[sub, H:H + T, :],
                         jnp.where(lane < 128, s4_scr[sub, H:H + T, :],
                                   jnp.where(lane < 192, s8_scr[sub, H:H + T, :], s16)))
        pos = t * T + lax.broadcasted_iota(jnp.int32, (T, 1), 0)
        cnt = jnp.minimum(pos + 1, wlane).astype(F32)
        pooled = (wsum / cnt - pv).astype(BF16)
        pool_out = _nn(pooled, pw_ref[...]) * ps_ref[...]
        ypool_ref[rows, :] = (pool_out * _silu(proj(5))).astype(BF16)

        q_ref[rows, :] = _head_rms(proj(6), gq_ref[...], bd) * (HEAD_DIM ** -0.5)
        k = _head_rms(proj(7), gk_ref[...], bd)
        v = proj(8)
        vt = v.T
        kt_ref[n_prev, :, rows] = k.T
        vt_ref[n_prev, :, rows] = vt
        kmean_ref[sub] = jnp.mean(k, axis=0, keepdims=True)
        onehot = (lane64 == t).astype(F32)
        kparts = []
        for hd in range(N_HEADS):
            kparts += [k[:, hd * HEAD_DIM:(hd + 1) * HEAD_DIM],
                       jnp.where(slope_lanes, _slope(hd) * key_idx, onehot)]
        kaug_ref[sub] = jnp.concatenate(kparts, axis=1).astype(BF16)
        vaug_ref[sub] = _value_groups(vt, T)
        az_ref[rows, :] = _silu(proj(9)).astype(BF16)

        mqt = (_head_rms(proj(10), gmq_ref[...], bd) * (HEAD_DIM ** -0.5 * LOG2E)).T.astype(BF16)
        for hd in range(N_HEADS):
            hs = slice(hd * HEAD_DIM, (hd + 1) * HEAD_DIM)
            s = _nn(memk_ref[:, hs], mqt[hs, :])
            p = jnp.exp2(s - jnp.max(s, axis=0, keepdims=True))
            acc = _nn(memvaug_ref[hd * V_AUG_ROWS:(hd + 1) * V_AUG_ROWS, :], p.astype(BF16))
            ot_scr[sub, hs, :] = acc[0:HEAD_DIM] / acc[HEAD_DIM:HEAD_DIM + 1]
        ymem_ref[rows, :] = (ot_scr[sub].T * _silu(proj(11))).astype(BF16)

    last = NT * T
    cstate_ref[...] = u_scr[CONV_HALO + last - 2:CONV_HALO + last, :]
    pstate_ref[...] = e_scr[H + last - POOL_STATE:H + last, :]
    u_scr[0:CONV_HALO, :] = u_scr[last:last + CONV_HALO, :]
    e_scr[0:H, :] = e_scr[last:last + H, :]


def _prompt_front(x, lw, memk_bf, memvaug, kt_prev, vt_prev):
    batch, seq, _ = x.shape
    nt = seq // ROW_TILE
    NT = FRONT_TILES
    R = NT * ROW_TILE
    assert nt <= MAX_BLOCKS and nt % NT == 0
    n_prev = 0 if kt_prev is None else kt_prev.shape[0]
    row = lambda b, t: (b, t, 0)
    const2 = lambda b, t: (0, 0)
    tile = lambda dt: jax.ShapeDtypeStruct((batch, seq, W_GROUP), dt)
    tile_t = jax.ShapeDtypeStruct((n_prev + 1, batch, W_GROUP, seq), F32)
    tile_spec = pl.BlockSpec((None, R, W_GROUP), row)
    planes_spec = lambda n: pl.BlockSpec((n, None, W_GROUP, R), lambda b, t: (0, b, 0, t))
    tile_t_spec = planes_spec(n_prev + 1)
    kaug_spec = pl.BlockSpec((None, NT, ROW_TILE, K_AUG_LANES), lambda b, t: (b, t, 0, 0))
    vaug_spec = pl.BlockSpec((None, NT, N_HEADS * V_AUG_ROWS, ROW_TILE), lambda b, t: (b, t, 0, 0))
    vec = pl.BlockSpec((1, W_GROUP), const2)
    prev_specs = [planes_spec(n_prev)] * 2 if n_prev else []
    prev_args = (kt_prev, vt_prev) if n_prev else ()
    return pl.pallas_call(
        functools.partial(_prompt_front_kernel, n_prev),
        grid=(batch, nt // NT),
        in_specs=[
            pl.BlockSpec((None, R, D_MODEL), row),
            pl.BlockSpec((1, D_MODEL), const2),
            pl.BlockSpec((D_MODEL, N_IN_SPLITS * W_GROUP), const2),
            pl.BlockSpec((CONV_W, W_GROUP), const2),
            vec,
            pl.BlockSpec((W_GROUP, W_GROUP), const2),
            vec, vec, vec, vec,
            pl.BlockSpec((W_GROUP, W_GROUP), const2),
            pl.BlockSpec((None, N_MEM, W_GROUP), lambda b, t: (b, 0, 0)),
            pl.BlockSpec((None, N_HEADS * V_AUG_ROWS, N_MEM), lambda b, t: (b, 0, 0)),
        ] + prev_specs,
        out_specs=[
            tile_spec, tile_t_spec, tile_t_spec, kaug_spec, vaug_spec,
            pl.BlockSpec((None, NT, 1, W_GROUP), lambda b, t: (b, t, 0, 0)),
            tile_spec, tile_spec, tile_spec, tile_spec,
            pl.BlockSpec((None, CONV_W - 1, W_GROUP), lambda b, t: (b, 0, 0)),
            pl.BlockSpec((None, POOL_STATE, W_GROUP), lambda b, t: (b, 0, 0)),
        ],
        out_shape=[
            tile(F32), tile_t, tile_t,
            jax.ShapeDtypeStruct((batch, nt, ROW_TILE, K_AUG_LANES), BF16),
            jax.ShapeDtypeStruct((batch, nt, N_HEADS * V_AUG_ROWS, ROW_TILE), BF16),
            jax.ShapeDtypeStruct((batch, nt, 1, W_GROUP), F32),
            tile(BF16), tile(BF16), tile(BF16), tile(BF16),
            jax.ShapeDtypeStruct((batch, CONV_W - 1, W_GROUP), F32),
            jax.ShapeDtypeStruct((batch, POOL_STATE, W_GROUP), F32),
        ],
        scratch_shapes=[
            pltpu.VMEM((CONV_HALO + R, W_GROUP), F32),
            pltpu.VMEM((POOL_HALO + R, W_GROUP), F32),
            pltpu.VMEM((NT, POOL_HALO + ROW_TILE, W_GROUP), F32),
            pltpu.VMEM((NT, POOL_HALO + ROW_TILE, W_GROUP), F32),
            pltpu.VMEM((NT, POOL_HALO + ROW_TILE, W_GROUP), F32),
            pltpu.VMEM((NT, ROW_TILE, N_IN_SPLITS * W_GROUP), F32),
            pltpu.VMEM((NT, W_GROUP, ROW_TILE), F32),
        ],
        compiler_params=pltpu.CompilerParams(dimension_semantics=("arbitrary", "arbitrary"),
                                             vmem_limit_bytes=VMEM_LIMIT),
        name="prompt_front",
    )(x, lw["g_norm"], lw["w_in"], lw["conv_w"], lw["conv_b"], lw["pool_w"], lw["pool_scale"],
      lw["g_q"], lw["g_k"], lw["g_mq"], lw["bd"], memk_bf, memvaug, *prev_args)


def _prompt_moba_kernel(q_ref, kaug_ref, vaug_ref, kmean_ref, az_ref, x_ref, yconv_ref, ypool_ref, ymem_ref, wout_ref,
                        y_ref, qaug_scr, acc_scr, outt_scr, s_scr):
    step = pl.program_id(1)
    T = ROW_TILE
    MT = MOBA_TILES
    nb = kmean_ref.shape[0]
    km = kmean_ref[...]
    blk_iota = lax.broadcasted_iota(jnp.int32, (nb, T), 0)
    blk_f = blk_iota.astype(F32)
    row8 = lax.broadcasted_iota(jnp.int32, (8, T), 0)
    log2e_rows = jnp.where(row8 == 0, LOG2E_HI, jnp.where(row8 == 1, LOG2E_LO, 0.0))
    pad_rows = jnp.zeros((K_GROUP_LANES - HEAD_DIM - MAX_BLOCKS - 8, T), F32)
    causal = lax.broadcasted_iota(jnp.int32, (T, T), 0) <= lax.broadcasted_iota(jnp.int32, (T, T), 1)
    chains = [(tile, hd) for tile in range(MT) for hd in range(N_HEADS)]

    def tile_block(tile):
        return step * MT + tile

    def kgroup(n, hd):
        return kaug_ref[n, :, hd * K_GROUP_LANES:(hd + 1) * K_GROUP_LANES]

    def vgroup(n, hd):
        return vaug_ref[n, hd * V_AUG_ROWS:(hd + 1) * V_AUG_ROWS, :]

    def block_term(n, i, hd):
        return (n - i).astype(F32) * (_slope(hd) * MOBA_BLOCK * LOG2E)

    def online_update(s, mx, cn, m, acc, vg):
        m_new = jnp.maximum(m, mx + cn)
        p = jnp.exp2(s - (m_new - cn))
        return m_new, jnp.exp2(m - m_new) * acc + _nn(vg, p.astype(BF16))

    m0, mx0 = [], []
    for tile in range(MT):
        i = tile_block(tile)
        i_f = i.astype(F32)
        qt = q_ref[tile * T:(tile + 1) * T, :].T
        for hd in range(N_HEADS):
            c = tile * N_HEADS + hd
            hs = slice(hd * HEAD_DIM, (hd + 1) * HEAD_DIM)
            qt_h = qt[hs, :]
            gate = jnp.where(blk_f < i_f, _dot3(km[:, hs], qt_h, _nn), NEG_INF)
            sel = _top3_mask(gate, blk_f, i_f, axis=0)
            bias = jnp.where((sel > 0.0) | (blk_iota == i), 0.0, MASKED)
            if nb < MAX_BLOCKS:
                bias = jnp.concatenate([bias, jnp.zeros((MAX_BLOCKS - nb, T), F32)], axis=0)
            qaug_scr[c] = jnp.concatenate([qt_h * LOG2E, bias, log2e_rows, pad_rows], axis=0).astype(BF16)
            s = jnp.where(causal, _nn(kgroup(i, hd), qaug_scr[c]), NEG_INF)
            m = jnp.max(s, axis=0, keepdims=True)
            acc_scr[c] = _nn(vgroup(i, hd), jnp.exp2(s - m).astype(BF16))
            m0.append(m)
            s = _nn(kgroup(0, hd), qaug_scr[c])
            s_scr[0, c] = s
            mx0.append(jnp.max(s, axis=0, keepdims=True))

    def stage(n, cur, nxt, ms, mxs):
        ms_out, mxs_out = [], []
        for c, (tile, hd) in enumerate(chains):
            i = tile_block(tile)
            s_next = _nn(kgroup(jnp.minimum(n + 1, i), hd), qaug_scr[c])
            s_scr[nxt, c] = s_next
            mxs_out.append(jnp.max(s_next, axis=0, keepdims=True))
            cn = jnp.where(n < i, block_term(n, i, hd), NEG_INF)
            m_new, acc_scr[c] = online_update(s_scr[cur, c], mxs[c], cn, ms[c], acc_scr[c],
                                              vgroup(jnp.minimum(n, i), hd))
            ms_out.append(m_new)
        return tuple(ms_out), tuple(mxs_out)

    def body(j, carry):
        ms, mxs = stage(2 * j, 0, 1, *carry)
        return stage(2 * j + 1, 1, 0, ms, mxs)

    lax.fori_loop(0, (tile_block(MT - 1) + 1) // 2, body, (tuple(m0), tuple(mx0)))
    for tile in range(MT):
        rows = slice(tile * T, (tile + 1) * T)
        for hd in range(N_HEADS):
            acc = acc_scr[tile * N_HEADS + hd]
            outt_scr[tile, hd * HEAD_DIM:(hd + 1) * HEAD_DIM, :] = acc[0:HEAD_DIM] / acc[HEAD_DIM:HEAD_DIM + 1]
        yatt = (outt_scr[tile].T * az_ref[rows, :].astype(F32)).astype(BF16)
        y = x_ref[rows, :]
        for g, part in enumerate((yconv_ref[rows, :], ypool_ref[rows, :], yatt, ymem_ref[rows, :])):
            y = y + _nn(part, wout_ref[g * W_GROUP:(g + 1) * W_GROUP, :])
        y_ref[rows, :] = y


def _prompt_moba(q, kaug, vaug, kmean, az, x, yconv, ypool, ymem, w_out_bf):
    batch, seq, _ = q.shape
    nt = seq // ROW_TILE
    MT = MOBA_TILES
    assert nt % MT == 0
    R = MT * ROW_TILE
    row = lambda b, t: (b, t, 0)
    part = pl.BlockSpec((None, R, W_GROUP), row)
    return pl.pallas_call(
        _prompt_moba_kernel,
        grid=(batch, nt // MT),
        in_specs=[
            part,
            pl.BlockSpec((None, nt, ROW_TILE, K_AUG_LANES), lambda b, t: (b, 0, 0, 0)),
            pl.BlockSpec((None, nt, N_HEADS * V_AUG_ROWS, ROW_TILE), lambda b, t: (b, 0, 0, 0)),
            pl.BlockSpec((None, nt, W_GROUP), lambda b, t: (b, 0, 0)),
            part,
            pl.BlockSpec((None, R, D_MODEL), row),
            part, part, part,
            pl.BlockSpec((D_MODEL, D_MODEL), lambda b, t: (0, 0)),
        ],
        out_specs=pl.BlockSpec((None, R, D_MODEL), row),
        out_shape=jax.ShapeDtypeStruct((batch, seq, D_MODEL), F32),
        scratch_shapes=[
            pltpu.VMEM((MT * N_HEADS, K_GROUP_LANES, ROW_TILE), BF16),
            pltpu.VMEM((MT * N_HEADS, V_AUG_ROWS, ROW_TILE), F32),
            pltpu.VMEM((MT, W_GROUP, ROW_TILE), F32),
            pltpu.VMEM((2, MT * N_HEADS, MOBA_BLOCK, ROW_TILE), F32),
        ],
        compiler_params=pltpu.CompilerParams(dimension_semantics=("arbitrary", "arbitrary"),
                                             vmem_limit_bytes=VMEM_LIMIT),
        name="prompt_moba",
    )(q, kaug, vaug, kmean, az, x, yconv, ypool, ymem, w_out_bf)


def _out_kernel(x_ref, yconv_ref, ypool_ref, yatt_ref, ymem_ref, w_ref, y_ref):
    acc = x_ref[...]
    for g, ref in enumerate((yconv_ref, ypool_ref, yatt_ref, ymem_ref)):
        acc = acc + _nn(ref[...].astype(BF16), w_ref[g * W_GROUP:(g + 1) * W_GROUP, :])
    y_ref[...] = acc


def _out_proj(x2, yconv, ypool, yatt, ymem, w_out_bf):
    rows = x2.shape[0]
    row = lambda r: (r, 0)
    part = pl.BlockSpec((ROW_TILE, W_GROUP), row)
    return pl.pallas_call(
        _out_kernel,
        grid=(rows // ROW_TILE,),
        in_specs=[pl.BlockSpec((ROW_TILE, D_MODEL), row), part, part, part, part,
                  pl.BlockSpec((D_MODEL, D_MODEL), lambda r: (0, 0))],
        out_specs=pl.BlockSpec((ROW_TILE, D_MODEL), row),
        out_shape=jax.ShapeDtypeStruct((rows, D_MODEL), F32),
        compiler_params=pltpu.CompilerParams(dimension_semantics=("arbitrary",),
                                             vmem_limit_bytes=VMEM_LIMIT),
        name="out_proj",
    )(x2, yconv, ypool, yatt, ymem, w_out_bf)


def _sample_front_kernel(start, x_ref, gn_ref, win_ref, cw_ref, cb_ref, pw_ref, ps_ref, gq_ref, gk_ref, gmq_ref,
                         bd_ref, sconv_ref, spool_ref,
                         q_ref, k_ref, v_ref, mq_ref, az_ref, mz_ref, yconv_ref, ypool_ref, cstate_ref, pstate_ref,
                         u_scr, e_scr):
    nb, ts = sconv_ref.shape[0], u_scr.shape[1] - CONV_HALO
    rows = nb * ts
    h = _rms(x_ref[...], gn_ref[...]).astype(BF16)
    bd = bd_ref[...]

    def proj(g):
        c = SPLIT_COLUMN[g]
        return _nn(h, win_ref[:, c * W_GROUP:(c + 1) * W_GROUP])

    def to3(a):
        return a.reshape(nb, ts, W_GROUP)

    def to2(a):
        return a.reshape(rows, W_GROUP)

    u = proj(2) * proj(0)
    u_scr[:, 0:CONV_HALO - 2, :] = jnp.zeros((nb, CONV_HALO - 2, W_GROUP), F32)
    u_scr[:, CONV_HALO - 2:CONV_HALO, :] = sconv_ref[...]
    u_scr[:, CONV_HALO:CONV_HALO + ts, :] = to3(u)
    cw = cw_ref[...]
    conv3 = (cw[0:1] * u_scr[:, CONV_HALO - 2:CONV_HALO - 2 + ts, :]
             + cw[1:2] * u_scr[:, CONV_HALO - 1:CONV_HALO - 1 + ts, :]
             + cw[2:3] * u_scr[:, CONV_HALO:CONV_HALO + ts, :])
    conv = cb_ref[...] + to2(conv3)
    yconv_ref[...] = proj(1) * conv * _silu(proj(3))
    cstate_ref[...] = u_scr[:, CONV_HALO + ts - 2:CONV_HALO + ts, :]

    H = 16
    pv = proj(4)
    e_scr[:, 0:1, :] = jnp.zeros((nb, 1, W_GROUP), F32)
    e_scr[:, 1:H, :] = spool_ref[...]
    e_scr[:, H:H + ts, :] = to3(pv)
    lane, wlane = _pool_lane_consts()
    wsum = jnp.zeros((nb, ts, W_GROUP), F32)
    for j in range(max(POOL_WINDOWS)):
        shifted = e_scr[:, H - j:H - j + ts, :]
        wsum = wsum + (shifted if j < min(POOL_WINDOWS) else jnp.where(wlane > j, shifted, 0.0))
    pos = start + lax.broadcasted_iota(jnp.int32, (ts, 1), 0)
    cnt = jnp.minimum(pos + 1, wlane).astype(F32)
    pooled = (to2(wsum / cnt) - pv).astype(BF16)
    pool_out = _nn(pooled, pw_ref[...]) * ps_ref[...]
    ypool_ref[...] = pool_out * _silu(proj(5))
    pstate_ref[...] = e_scr[:, H + ts - POOL_STATE:H + ts, :]

    q_ref[...] = _head_rms(proj(6), gq_ref[...], bd) * (HEAD_DIM ** -0.5)
    k_ref[...] = _head_rms(proj(7), gk_ref[...], bd)
    v_ref[...] = proj(8)
    az_ref[...] = _silu(proj(9))
    mq_ref[...] = _head_rms(proj(10), gmq_ref[...], bd) * (HEAD_DIM ** -0.5)
    mz_ref[...] = _silu(proj(11))


def _sample_front(x2, lw, sconv, spool, start):
    nb = sconv.shape[0]
    rows = x2.shape[0]
    ts = rows // nb
    full = lambda shape: pl.BlockSpec(shape, lambda i, n=len(shape): (0,) * n)
    t2 = jax.ShapeDtypeStruct((rows, W_GROUP), F32)
    return pl.pallas_call(
        functools.partial(_sample_front_kernel, start),
        grid=(1,),
        in_specs=[
            full((rows, D_MODEL)), full((1, D_MODEL)), full((D_MODEL, N_IN_SPLITS * W_GROUP)),
            full((CONV_W, W_GROUP)), full((1, W_GROUP)), full((W_GROUP, W_GROUP)),
            full((1, W_GROUP)), full((1, W_GROUP)), full((1, W_GROUP)), full((1, W_GROUP)),
            full((W_GROUP, W_GROUP)),
            full((nb, CONV_W - 1, W_GROUP)), full((nb, POOL_STATE, W_GROUP)),
        ],
        out_specs=[full((rows, W_GROUP))] * 8 + [full((nb, CONV_W - 1, W_GROUP)), full((nb, POOL_STATE, W_GROUP))],
        out_shape=[t2] * 8 + [jax.ShapeDtypeStruct((nb, CONV_W - 1, W_GROUP), F32),
                              jax.ShapeDtypeStruct((nb, POOL_STATE, W_GROUP), F32)],
        scratch_shapes=[
            pltpu.VMEM((nb, CONV_HALO + ts, W_GROUP), F32),
            pltpu.VMEM((nb, 16 + ts, W_GROUP), F32),
        ],
        compiler_params=pltpu.CompilerParams(dimension_semantics=("arbitrary",),
                                             vmem_limit_bytes=VMEM_LIMIT),
        name="sample_front",
    )(x2, lw["g_norm"], lw["w_in"], lw["conv_w"], lw["conv_b"], lw["pool_w"], lw["pool_scale"],
      lw["g_q"], lw["g_k"], lw["g_mq"], lw["bd"], sconv, spool)


def _head_rows(a, ts):
    r = lax.broadcasted_iota(jnp.int32, (N_HEADS * ts, W_GROUP), 0) // ts
    c = lax.broadcasted_iota(jnp.int32, (N_HEADS * ts, W_GROUP), 1) // HEAD_DIM
    return jnp.where(r == c, jnp.concatenate([a] * N_HEADS, axis=0), 0.0)


def _head_diag(o, ts):
    r = lax.broadcasted_iota(jnp.int32, (N_HEADS * ts, W_GROUP), 0) // ts
    c = lax.broadcasted_iota(jnp.int32, (N_HEADS * ts, W_GROUP), 1) // HEAD_DIM
    o = jnp.where(r == c, o, 0.0)
    out = o[0:ts]
    for hd in range(1, N_HEADS):
        out = out + o[hd * ts:(hd + 1) * ts]
    return out


def _sample_moba_kernel(layer, past_len, pt_ref, q_ref, knew_ref, vnew_ref, mq_ref, az_ref, mz_ref, memkt_ref,
                        memvt_ref, ck_hbm, cv_hbm, yatt_ref, ymem_ref, buf, sem, s_scr, p_scr):
    P = PAGES_PER_STEP
    BPS = P // PAGES_PER_BLOCK
    b = pl.program_id(0)
    n_seqs = pl.num_programs(0)
    ts = q_ref.shape[0]
    R = N_HEADS * ts
    n_pages = past_len // PAGE_SIZE
    nblk = past_len // MOBA_BLOCK
    nc = n_pages // P
    n_chunks = 2 * nc

    def chunk_copies(seq, g):
        src = ck_hbm if g < nc else cv_hbm
        first = (g % nc) * P
        slot = g % CHUNK_BUFS
        return [pltpu.make_async_copy(src.at[layer, pt_ref[seq, first + j]], buf.at[slot, j], sem.at[slot])
                for j in range(P)]

    def start(seq, g):
        for j, cp in enumerate(chunk_copies(seq, g)):
            cp.start(priority=j % 2)

    def wait(seq, g):
        for cp in chunk_copies(seq, g):
            cp.wait()

    def prefetch_after(g):
        ahead = g + CHUNK_BUFS - 1
        if ahead < n_chunks:
            start(b, ahead)
        else:
            @pl.when(b + 1 < n_seqs)
            def _():
                start(b + 1, ahead - n_chunks)

    @pl.when(b == 0)
    def _():
        for g in range(CHUNK_BUFS - 1):
            start(b, g)

    qrows = _head_rows(q_ref[...], ts)
    row_t = (lax.broadcasted_iota(jnp.int32, (R, 1), 0) % ts).astype(F32)
    row_h = lax.broadcasted_iota(jnp.int32, (R, 1), 0) // ts
    slope = jnp.where(row_h == 0, _slope(0), jnp.where(row_h == 1, _slope(1),
                                                        jnp.where(row_h == 2, _slope(2), _slope(3))))

    qb = qrows.astype(BF16)
    lane8 = lax.broadcasted_iota(jnp.int32, (W_GROUP, BPS), 1)
    ksums = []
    for g in range(nc):
        prefetch_after(g)
        wait(b, g)
        sums = jnp.zeros((W_GROUP, BPS), F32)
        for j in range(0, P, PAGES_PER_BLOCK):
            ka, kb = buf[g % CHUNK_BUFS, j], buf[g % CHUNK_BUFS, j + 1]
            s_scr[g * P + j] = _nn(qb, ka.astype(BF16))
            s_scr[g * P + j + 1] = _nn(qb, kb.astype(BF16))
            tot = jnp.sum(ka + kb, axis=1, keepdims=True)
            sums = jnp.where(lane8 == j // PAGES_PER_BLOCK, tot, sums)
        ksums.append(sums)

    gate = jnp.concatenate([_dot3(qrows, ks * (1.0 / MOBA_BLOCK), _nn) for ks in ksums], axis=1)
    blk_iota = lax.broadcasted_iota(jnp.int32, (R, nblk), 1).astype(F32)
    sel = _top3_mask(gate, blk_iota, float(nblk), axis=-1)
    blk_term = jnp.where(sel > 0.0, slope * (blk_iota * MOBA_BLOCK - past_len), NEG_INF)
    col = lax.broadcasted_iota(jnp.int32, (R, PAGE_SIZE), 1).astype(F32)
    page_term = [slope * (col + float(k * PAGE_SIZE)) for k in range(PAGES_PER_BLOCK)]
    own = []
    m = jnp.full((R, 1), NEG_INF, F32)
    for j in range(ts):
        sj = jnp.sum(qrows * knew_ref[j:j + 1, :], axis=-1, keepdims=True) + slope * float(j)
        sj = jnp.where(row_t >= float(j), sj, NEG_INF)
        own.append(sj)
        m = jnp.maximum(m, sj)
    mvec = jnp.full((R, PAGE_SIZE), NEG_INF, F32)
    for n in range(nblk):
        for k in range(PAGES_PER_BLOCK):
            pg = n * PAGES_PER_BLOCK + k
            s = s_scr[pg] + page_term[k] + blk_term[:, n:n + 1]
            s_scr[pg] = s
            mvec = jnp.maximum(mvec, s)
    m = jnp.maximum(m, jnp.max(mvec, axis=-1, keepdims=True))
    p_own = [jnp.exp(sj - m) for sj in own]
    lvec = jnp.zeros((R, PAGE_SIZE), F32)
    for pg in range(n_pages):
        p = jnp.exp(s_scr[pg] - m)
        p_scr[pg] = p.astype(BF16)
        lvec = lvec + p
    l = sum(p_own) + jnp.sum(lvec, axis=-1, keepdims=True)

    acc = jnp.zeros((R, W_GROUP), F32)
    for g in range(nc, n_chunks):
        prefetch_after(g)
        wait(b, g)
        for j in range(P):
            acc = acc + _nt(p_scr[(g - nc) * P + j], buf[g % CHUNK_BUFS, j].astype(BF16))
    for j in range(ts):
        acc = acc + p_own[j] * vnew_ref[j:j + 1, :]
    yatt_ref[...] = _head_diag(acc / l, ts) * az_ref[...]

    mrows = _head_rows(mq_ref[...], ts).astype(BF16)
    s = _nn(mrows, memkt_ref[...].astype(BF16))
    p = jnp.exp(s - jnp.max(s, axis=-1, keepdims=True))
    lm = jnp.sum(p, axis=-1, keepdims=True)
    o = _nt(p.astype(BF16), memvt_ref[...].astype(BF16)) / lm
    ymem_ref[...] = _head_diag(o, ts) * mz_ref[...]


def _sample_moba(layer, page_table, q, knew, vnew, mq, az, mz, memkt, memvt, cache_kt, cache_vt):
    nb, n_pages = page_table.shape
    rows = q.shape[0]
    ts = rows // nb
    P = PAGES_PER_STEP
    assert n_pages % P == 0 and P % PAGES_PER_BLOCK == 0
    assert (2 * n_pages // P) % CHUNK_BUFS == 0
    past_len = n_pages * PAGE_SIZE
    R = N_HEADS * ts
    seq = pl.BlockSpec((ts, W_GROUP), lambda b, pt: (b, 0))
    mem = pl.BlockSpec((None, None, W_GROUP, N_MEM), lambda b, pt: (layer, b, 0, 0))
    hbm = pl.BlockSpec(memory_space=pl.ANY)
    out = jax.ShapeDtypeStruct((rows, W_GROUP), F32)
    grid_spec = pltpu.PrefetchScalarGridSpec(
        num_scalar_prefetch=1,
        grid=(nb,),
        in_specs=[seq, seq, seq, seq, seq, seq, mem, mem, hbm, hbm],
        out_specs=[seq, seq],
        scratch_shapes=[
            pltpu.VMEM((CHUNK_BUFS, P, W_GROUP, PAGE_SIZE), F32),
            pltpu.SemaphoreType.DMA((CHUNK_BUFS,)),
            pltpu.VMEM((n_pages, R, PAGE_SIZE), F32),
            pltpu.VMEM((n_pages, R, PAGE_SIZE), BF16),
        ],
    )
    return pl.pallas_call(
        functools.partial(_sample_moba_kernel, layer, past_len),
        grid_spec=grid_spec,
        out_shape=[out, out],
        compiler_params=pltpu.CompilerParams(dimension_semantics=("arbitrary",),
                                             vmem_limit_bytes=VMEM_LIMIT),
        name="sample_moba",
    )(page_table, q, knew, vnew, mq, az, mz, memkt, memvt, cache_kt, cache_vt)


def _same_head_matrix():
    r = lax.broadcasted_iota(jnp.int32, (W_GROUP, W_GROUP), 0) // HEAD_DIM
    c = lax.broadcasted_iota(jnp.int32, (W_GROUP, W_GROUP), 1) // HEAD_DIM
    return (r == c).astype(BF16)


def _pool_block_diag(pool_w_l):
    out = jnp.zeros((W_GROUP, W_GROUP), pool_w_l.dtype)
    for g in range(len(POOL_WINDOWS)):
        out = out.at[g * 64:(g + 1) * 64, g * 64:(g + 1) * 64].set(pool_w_l[g])
    return out


def _tokens_minor(a):
    lead = a.shape[:-3]
    n = len(lead)
    perm = tuple(range(n)) + (n + 1, n + 2, n)
    return a.transpose(perm).reshape(lead + (W_GROUP, a.shape[-3]))


def _tokens_major(a_t):
    lead = a_t.shape[:-2]
    n = len(lead)
    perm = tuple(range(n)) + (n + 2, n, n + 1)
    return a_t.reshape(lead + (N_HEADS, HEAD_DIM, a_t.shape[-1])).transpose(perm)


def kernel(x_prompt, x_sample, cache_k, cache_v, page_table, state_conv, state_pool, cache_mem_k, cache_mem_v,
           mem_prompt, g_norm, w_in, w_out, conv_w, conv_b, pool_w, pool_scale, g_q, g_k, g_mq, g_mk, g_mem,
           w_mem_kv):
    depth = w_in.shape[0]
    bp, seq, _ = x_prompt.shape
    bs, ts, _ = x_sample.shape
    n_pages = page_table.shape[1]
    past_len = n_pages * PAGE_SIZE
    bd = _same_head_matrix()
    tile4 = lambda g: jnp.tile(g, (1, N_HEADS))[:, None, :]

    w_in_bf = jnp.concatenate([w_in[:, :, g * W_GROUP:(g + 1) * W_GROUP].astype(BF16) for g in SPLIT_ORDER], axis=2)
    w_out_bf = w_out.astype(BF16)
    g_q_t, g_k_t, g_mq_t, g_mk_t = tile4(g_q), tile4(g_k), tile4(g_mq), tile4(g_mk)
    layer_w = []
    for l in range(depth):
        layer_w.append(dict(
            g_norm=g_norm[l][None, :], w_in=w_in_bf[l], conv_w=conv_w[l], conv_b=conv_b[l][None, :],
            pool_w=_pool_block_diag(pool_w[l]).astype(BF16), pool_scale=pool_scale[l][None, :],
            g_q=g_q_t[l], g_k=g_k_t[l], g_mq=g_mq_t[l], bd=bd))

    mkt_all, mvt_all, mk_bf, mvaug = _memkv(mem_prompt, g_mem[:, None, :], w_mem_kv.astype(BF16), g_mk_t, bd)

    cache_kt = _tokens_minor(cache_k)
    cache_vt = _tokens_minor(cache_v)
    memkt_s = _tokens_minor(cache_mem_k)
    memvt_s = _tokens_minor(cache_mem_v)

    xp = x_prompt
    xs = x_sample.reshape(bs * ts, D_MODEL)
    ks_l, vs_l, cp_l, cs_l, pp_l, ps_l = ([] for _ in range(6))
    kt, vt = None, None
    for l in range(depth):
        lw = layer_w[l]
        (q, kt, vt, kaug, vaug, kmean, yconv, ypool, ymem, az, cst, pst) = _prompt_front(
            xp, lw, mk_bf[l], mvaug[l], kt, vt)
        xp = _prompt_moba(q, kaug, vaug, kmean.reshape(bp, seq // ROW_TILE, W_GROUP), az,
                          xp, yconv, ypool, ymem, w_out_bf[l])
        cp_l.append(cst)
        pp_l.append(pst)
        (qs, ksn, vsn, mqs, azs, mzs, yconv_s, ypool_s, cst_s, pst_s) = _sample_front(
            xs, lw, state_conv[l], state_pool[l], past_len)
        yatt_s, ymem_s = _sample_moba(l, page_table, qs, ksn, vsn, mqs, azs, mzs, memkt_s, memvt_s,
                                      cache_kt, cache_vt)
        xs = _out_proj(xs, yconv_s, ypool_s, yatt_s, ymem_s, w_out_bf[l])
        ks_l.append(ksn.reshape(bs, ts, N_HEADS, HEAD_DIM))
        vs_l.append(vsn.reshape(bs, ts, N_HEADS, HEAD_DIM))
        cs_l.append(cst_s)
        ps_l.append(pst_s)

    return (xp, xs.reshape(bs, ts, D_MODEL), _tokens_major(kt), _tokens_major(vt), jnp.stack(ks_l), jnp.stack(vs_l),
            jnp.stack(cp_l), jnp.stack(cs_l), jnp.stack(pp_l), jnp.stack(ps_l),
            _tokens_major(mkt_all), _tokens_major(mvt_all))
```

```python
import functools
import struct

import jax
import jax.numpy as jnp
from jax import lax
from jax.experimental import pallas as pl
from jax.experimental.pallas import tpu as pltpu

F32 = jnp.float32
BF16 = jnp.bfloat16

D_MODEL = 1024
W_GROUP = 256
N_HEADS = 4
HEAD_DIM = 64
N_MEM = 256
CONV_W = 3
POOL_WINDOWS = (2, 4, 8, 16)
POOL_STATE = 15
MOBA_BLOCK = 256
MOBA_TOPK = 3
PAGE_SIZE = 128
N_IN_SPLITS = 12
EPS = 1e-6
NEG_INF = float("-inf")

ROW_TILE = 256
FRONT_TILES = 4
MOBA_TILES = 2
POOL_HALO = 32
CONV_HALO = 8
PAGES_PER_STEP = 16
CHUNK_BUFS = 4
PAGES_PER_BLOCK = MOBA_BLOCK // PAGE_SIZE
VMEM_LIMIT = 56 * 1024 * 1024

SPLIT_ORDER = (10, 7, 6, 8, 11, 9, 4, 5, 2, 0, 1, 3)
SPLIT_COLUMN = {g: c for c, g in enumerate(SPLIT_ORDER)}

MAX_BLOCKS = 32
AUG_LANES = 64
K_GROUP_LANES = HEAD_DIM + AUG_LANES
K_AUG_LANES = N_HEADS * K_GROUP_LANES
V_AUG_ROWS = 80
MASKED = -1e30


def _bf16_round(x):
    bits = struct.unpack("<I", struct.pack("<f", x))[0]
    bits = (bits + 0x7FFF + ((bits >> 16) & 1)) & 0xFFFF0000
    return struct.unpack("<f", struct.pack("<I", bits))[0]


LOG2E = 1.4426950408889634
LOG2E_HI = _bf16_round(LOG2E)
LOG2E_LO = LOG2E - LOG2E_HI


def _slope(h):
    return 2.0 ** (-8.0 * (h + 1) / N_HEADS)


def _nt(a, b):
    return lax.dot_general(a, b, (((1,), (1,)), ((), ())), preferred_element_type=F32)


def _split_bf16(x):
    hi = x.astype(BF16)
    lo = (x - hi.astype(F32)).astype(BF16)
    return hi, lo


def _dot3(a, b, dot):
    ah, al = _split_bf16(a)
    bh, bl = _split_bf16(b)
    return dot(ah, bh) + dot(ah, bl) + dot(al, bh)


def _nn(a, b):
    return jnp.dot(a, b, preferred_element_type=F32)


def _rms(x, g):
    ms = jnp.mean(x * x, axis=-1, keepdims=True)
    return x * lax.rsqrt(ms + EPS) * g


def _head_rms(x, g, bd):
    hi, lo = _split_bf16(x * x)
    ssq = _nn(hi, bd) + _nn(lo, bd)
    return x * lax.rsqrt(ssq * (1.0 / HEAD_DIM) + EPS) * g


def _silu(z):
    return z / (1.0 + jnp.exp(-z))


def _top3_mask(gate, blk_iota, limit, axis):
    sel = jnp.zeros(gate.shape, F32)
    g = gate
    for _ in range(MOBA_TOPK):
        m = jnp.max(g, axis=axis, keepdims=True)
        idx = jnp.min(jnp.where(g == m, blk_iota, 1e9), axis=axis, keepdims=True)
        pick = blk_iota == idx
        sel = jnp.where(pick & (idx < limit), 1.0, sel)
        g = jnp.where(pick, NEG_INF, g)
    return sel


def _pool_lane_consts():
    lane = lax.broadcasted_iota(jnp.int32, (1, W_GROUP), 1)
    w = jnp.where(lane < 64, 2, jnp.where(lane < 128, 4, jnp.where(lane < 192, 8, 16)))
    return lane, w


def _value_groups(vt, n_tok):
    ones_row = (lax.broadcasted_iota(jnp.int32, (V_AUG_ROWS - HEAD_DIM, n_tok), 0) == 0).astype(F32)
    parts = []
    for hd in range(N_HEADS):
        parts += [vt[hd * HEAD_DIM:(hd + 1) * HEAD_DIM, :], ones_row]
    return jnp.concatenate(parts, axis=0).astype(BF16)


def _memkv_kernel(mem_ref, g_ref, w_ref, gmk_ref, bd_ref, mkt_ref, mvt_ref, mkb_ref, mvaug_ref):
    h = _rms(mem_ref[...], g_ref[...]).astype(BF16)
    kv = _nn(h, w_ref[...])
    mk = _head_rms(kv[:, :W_GROUP], gmk_ref[...], bd_ref[...])
    mvt = kv[:, W_GROUP:].T
    mkt_ref[...] = mk.T
    mvt_ref[...] = mvt
    mkb_ref[...] = mk.astype(BF16)
    mvaug_ref[...] = _value_groups(mvt, N_MEM)


def _memkv(mem_prompt, g_mem, w_mem_kv_bf, g_mk_t, bd):
    depth = g_mem.shape[0]
    batch = mem_prompt.shape[0]
    out = lambda dt: jax.ShapeDtypeStruct((depth, batch, N_MEM, W_GROUP), dt)
    ospec = pl.BlockSpec((None, None, N_MEM, W_GROUP), lambda l, b: (l, b, 0, 0))
    vaug_rows = N_HEADS * V_AUG_ROWS
    return pl.pallas_call(
        _memkv_kernel,
        grid=(depth, batch),
        in_specs=[
            pl.BlockSpec((None, N_MEM, D_MODEL), lambda l, b: (b, 0, 0)),
            pl.BlockSpec((None, 1, D_MODEL), lambda l, b: (l, 0, 0)),
            pl.BlockSpec((None, D_MODEL, 2 * W_GROUP), lambda l, b: (l, 0, 0)),
            pl.BlockSpec((None, 1, W_GROUP), lambda l, b: (l, 0, 0)),
            pl.BlockSpec((W_GROUP, W_GROUP), lambda l, b: (0, 0)),
        ],
        out_specs=[ospec, ospec, ospec,
                   pl.BlockSpec((None, None, vaug_rows, N_MEM), lambda l, b: (l, b, 0, 0))],
        out_shape=[out(F32), out(F32), out(BF16),
                   jax.ShapeDtypeStruct((depth, batch, vaug_rows, N_MEM), BF16)],
        compiler_params=pltpu.CompilerParams(dimension_semantics=("arbitrary", "arbitrary")),
        name="memkv",
    )(mem_prompt, g_mem, w_mem_kv_bf, g_mk_t, bd)


def _prompt_front_kernel(n_prev, x_ref, gn_ref, win_ref, cw_ref, cb_ref, pw_ref, ps_ref, gq_ref, gk_ref, gmq_ref,
                         bd_ref, memk_ref, memvaug_ref, *refs):
    if n_prev:
        ktprev_ref, vtprev_ref, *refs = refs
    (q_ref, kt_ref, vt_ref, kaug_ref, vaug_ref, kmean_ref, yconv_ref, ypool_ref, ymem_ref, az_ref,
     cstate_ref, pstate_ref, u_scr, e_scr, s2_scr, s4_scr, s8_scr, proj_scr, ot_scr, ms_scr) = refs
    if n_prev:
        kt_ref[0:n_prev] = ktprev_ref[...]
        vt_ref[0:n_prev] = vtprev_ref[...]
    step = pl.program_id(1)
    T = ROW_TILE
    H = POOL_HALO
    NT = FRONT_TILES

    @pl.when(step == 0)
    def _():
        u_scr[0:CONV_HALO, :] = jnp.zeros((CONV_HALO, W_GROUP), F32)
        e_scr[0:H, :] = jnp.zeros((H, W_GROUP), F32)

    bd = bd_ref[...]
    cw = cw_ref[...]
    lane, wlane = _pool_lane_consts()
    lane64 = lax.broadcasted_iota(jnp.int32, (T, AUG_LANES), 1)
    key_idx = lax.broadcasted_iota(jnp.int32, (T, AUG_LANES), 0).astype(F32)
    slope_lanes = (lane64 == MAX_BLOCKS) | (lane64 == MAX_BLOCKS + 1)

    for sub in range(NT):
        t = step * NT + sub
        rows = slice(sub * T, (sub + 1) * T)

        h = _rms(x_ref[rows, :], gn_ref[...]).astype(BF16)
        proj_scr[sub] = _nn(h, win_ref[...])

        def proj(g, sub=sub):
            c = SPLIT_COLUMN[g]
            return proj_scr[sub, :, c * W_GROUP:(c + 1) * W_GROUP]

        ub = CONV_HALO + sub * T
        u = proj(2) * proj(0)
        u_scr[ub:ub + T, :] = u
        conv = (cb_ref[...] + cw[0:1] * u_scr[ub - 2:ub - 2 + T, :]
                + cw[1:2] * u_scr[ub - 1:ub - 1 + T, :] + cw[2:3] * u)
        yconv_ref[rows, :] = (proj(1) * conv * _silu(proj(3))).astype(BF16)

        eb = sub * T
        pv = proj(4)
        e_scr[eb + H:eb + H + T, :] = pv
        s2_scr[sub, 8:H + T, :] = e_scr[eb + 8:eb + H + T, :] + e_scr[eb + 7:eb + H + T - 1, :]
        s4_scr[sub, 16:H + T, :] = s2_scr[sub, 16:H + T, :] + s2_scr[sub, 14:H + T - 2, :]
        s8_scr[sub, 24:H + T, :] = s4_scr[sub, 24:H + T, :] + s4_scr[sub, 20:H + T - 4, :]
        s16 = s8_scr[sub, H:H + T, :] + s8_scr[sub, H - 8:H + T - 8, :]
        wsum = jnp.where(lane < 64, s2_scr[sub, H:H + T, :],
                         jnp.where(lane < 128, s4_scr[sub, H:H + T, :],
                                   jnp.where(lane < 192, s8_scr[sub, H:H + T, :], s16)))
        pos = t * T + lax.broadcasted_iota(jnp.int32, (T, 1), 0)
        cnt = jnp.minimum(pos + 1, wlane).astype(F32)
        pooled = (wsum / cnt - pv).astype(BF16)
        pool_out = _nn(pooled, pw_ref[...]) * ps_ref[...]
        ypool_ref[rows, :] = (pool_out * _silu(proj(5))).astype(BF16)

        q_ref[rows, :] = _head_rms(proj(6), gq_ref[...], bd) * (HEAD_DIM ** -0.5)
        k = _head_rms(proj(7), gk_ref[...], bd)
        v = proj(8)
        vt = v.T
        kt_ref[n_prev, :, rows] = k.T
        vt_ref[n_prev, :, rows] = vt
        kmean_ref[sub] = jnp.mean(k, axis=0, keepdims=True)
        onehot = (lane64 == t).astype(F32)
        kparts = []
        for hd in range(N_HEADS):
            kparts += [k[:, hd * HEAD_DIM:(hd + 1) * HEAD_DIM],
                       jnp.where(slope_lanes, _slope(hd) * key_idx, onehot)]
        kaug_ref[sub] = jnp.concatenate(kparts, axis=1).astype(BF16)
        vaug_ref[sub] = _value_groups(vt, T)
        az_ref[rows, :] = _silu(proj(9)).astype(BF16)

        mqt = (_head_rms(proj(10), gmq_ref[...], bd) * (HEAD_DIM ** -0.5 * LOG2E)).T.astype(BF16)
        mmax = []
        for hd in range(N_HEADS):
            s = _nn(memk_ref[:, hd * HEAD_DIM:(hd + 1) * HEAD_DIM], mqt[hd * HEAD_DIM:(hd + 1) * HEAD_DIM, :])
            ms_scr[sub, hd] = s
            mmax.append(jnp.max(s, axis=0, keepdims=True))
        for hd in range(N_HEADS):
            hs = slice(hd * HEAD_DIM, (hd + 1) * HEAD_DIM)
            p = jnp.exp2(ms_scr[sub, hd] - mmax[hd])
            acc = _nn(memvaug_ref[hd * V_AUG_ROWS:(hd + 1) * V_AUG_ROWS, :], p.astype(BF16))
            ot_scr[sub, hs, :] = acc[0:HEAD_DIM] / acc[HEAD_DIM:HEAD_DIM + 1]
        ymem_ref[rows, :] = (ot_scr[sub].T * _silu(proj(11))).astype(BF16)

    last = NT * T
    cstate_ref[...] = u_scr[CONV_HALO + last - 2:CONV_HALO + last, :]
    pstate_ref[...] = e_scr[H + last - POOL_STATE:H + last, :]
    u_scr[0:CONV_HALO, :] = u_scr[last:last + CONV_HALO, :]
    e_scr[0:H, :] = e_scr[last:last + H, :]


def _prompt_front(x, lw, memk_bf, memvaug, kt_prev, vt_prev):
    batch, seq, _ = x.shape
    nt = seq // ROW_TILE
    NT = FRONT_TILES
    R = NT * ROW_TILE
    assert nt <= MAX_BLOCKS and nt % NT == 0
    n_prev = 0 if kt_prev is None else kt_prev.shape[0]
    row = lambda b, t: (b, t, 0)
    const2 = lambda b, t: (0, 0)
    tile = lambda dt: jax.ShapeDtypeStruct((batch, seq, W_GROUP), dt)
    tile_t = jax.ShapeDtypeStruct((n_prev + 1, batch, W_GROUP, seq), F32)
    tile_spec = pl.BlockSpec((None, R, W_GROUP), row)
    planes_spec = lambda n: pl.BlockSpec((n, None, W_GROUP, R), lambda b, t: (0, b, 0, t))
    tile_t_spec = planes_spec(n_prev + 1)
    kaug_spec = pl.BlockSpec((None, NT, ROW_TILE, K_AUG_LANES), lambda b, t: (b, t, 0, 0))
    vaug_spec = pl.BlockSpec((None, NT, N_HEADS * V_AUG_ROWS, ROW_TILE), lambda b, t: (b, t, 0, 0))
    vec = pl.BlockSpec((1, W_GROUP), const2)
    prev_specs = [planes_spec(n_prev)] * 2 if n_prev else []
    prev_args = (kt_prev, vt_prev) if n_prev else ()
    return pl.pallas_call(
        functools.partial(_prompt_front_kernel, n_prev),
        grid=(batch, nt // NT),
        in_specs=[
            pl.BlockSpec((None, R, D_MODEL), row),
            pl.BlockSpec((1, D_MODEL), const2),
            pl.BlockSpec((D_MODEL, N_IN_SPLITS * W_GROUP), const2),
            pl.BlockSpec((CONV_W, W_GROUP), const2),
            vec,
            pl.BlockSpec((W_GROUP, W_GROUP), const2),
            vec, vec, vec, vec,
            pl.BlockSpec((W_GROUP, W_GROUP), const2),
            pl.BlockSpec((None, N_MEM, W_GROUP), lambda b, t: (b, 0, 0)),
            pl.BlockSpec((None, N_HEADS * V_AUG_ROWS, N_MEM), lambda b, t: (b, 0, 0)),
        ] + prev_specs,
        out_specs=[
            tile_spec, tile_t_spec, tile_t_spec, kaug_spec, vaug_spec,
            pl.BlockSpec((None, NT, 1, W_GROUP), lambda b, t: (b, t, 0, 0)),
            tile_spec, tile_spec, tile_spec, tile_spec,
            pl.BlockSpec((None, CONV_W - 1, W_GROUP), lambda b, t: (b, 0, 0)),
            pl.BlockSpec((None, POOL_STATE, W_GROUP), lambda b, t: (b, 0, 0)),
        ],
        out_shape=[
            tile(F32), tile_t, tile_t,
            jax.ShapeDtypeStruct((batch, nt, ROW_TILE, K_AUG_LANES), BF16),
            jax.ShapeDtypeStruct((batch, nt, N_HEADS * V_AUG_ROWS, ROW_TILE), BF16),
            jax.ShapeDtypeStruct((batch, nt, 1, W_GROUP), F32),
            tile(BF16), tile(BF16), tile(BF16), tile(BF16),
            jax.ShapeDtypeStruct((batch, CONV_W - 1, W_GROUP), F32),
            jax.ShapeDtypeStruct((batch, POOL_STATE, W_GROUP), F32),
        ],
        scratch_shapes=[
            pltpu.VMEM((CONV_HALO + R, W_GROUP), F32),
            pltpu.VMEM((POOL_HALO + R, W_GROUP), F32),
            pltpu.VMEM((NT, POOL_HALO + ROW_TILE, W_GROUP), F32),
            pltpu.VMEM((NT, POOL_HALO + ROW_TILE, W_GROUP), F32),
            pltpu.VMEM((NT, POOL_HALO + ROW_TILE, W_GROUP), F32),
            pltpu.VMEM((NT, ROW_TILE, N_IN_SPLITS * W_GROUP), F32),
            pltpu.VMEM((NT, W_GROUP, ROW_TILE), F32),
            pltpu.VMEM((NT, N_HEADS, N_MEM, ROW_TILE), F32),
        ],
        compiler_params=pltpu.CompilerParams(dimension_semantics=("arbitrary", "arbitrary"),
                                             vmem_limit_bytes=VMEM_LIMIT),
        name="prompt_front",
    )(x, lw["g_norm"], lw["w_in"], lw["conv_w"], lw["conv_b"], lw["pool_w"], lw["pool_scale"],
      lw["g_q"], lw["g_k"], lw["g_mq"], lw["bd"], memk_bf, memvaug, *prev_args)


def _prompt_moba_kernel(q_ref, kaug_ref, vaug_ref, kmean_ref, az_ref, x_ref, yconv_ref, ypool_ref, ymem_ref, wout_ref,
                        y_ref, qaug_scr, acc_scr, outt_scr, s_scr):
    step = pl.program_id(1)
    T = ROW_TILE
    MT = MOBA_TILES
    nb = kmean_ref.shape[0]
    km = kmean_ref[...]
    blk_iota = lax.broadcasted_iota(jnp.int32, (nb, T), 0)
    blk_f = blk_iota.astype(F32)
    row8 = lax.broadcasted_iota(jnp.int32, (8, T), 0)
    log2e_rows = jnp.where(row8 == 0, LOG2E_HI, jnp.where(row8 == 1, LOG2E_LO, 0.0))
    pad_rows = jnp.zeros((K_GROUP_LANES - HEAD_DIM - MAX_BLOCKS - 8, T), F32)
    causal = lax.broadcasted_iota(jnp.int32, (T, T), 0) <= lax.broadcasted_iota(jnp.int32, (T, T), 1)
    chains = [(tile, hd) for tile in range(MT) for hd in range(N_HEADS)]

    def tile_block(tile):
        return step * MT + tile

    def kgroup(n, hd):
        return kaug_ref[n, :, hd * K_GROUP_LANES:(hd + 1) * K_GROUP_LANES]

    def vgroup(n, hd):
        return vaug_ref[n, hd * V_AUG_ROWS:(hd + 1) * V_AUG_ROWS, :]

    def block_term(n, i, hd):
        return (n - i).astype(F32) * (_slope(hd) * MOBA_BLOCK * LOG2E)

    def online_update(s, mx, cn, m, acc, vg):
        m_new = jnp.maximum(m, mx + cn)
        p = jnp.exp2(s - (m_new - cn))
        return m_new, jnp.exp2(m - m_new) * acc + _nn(vg, p.astype(BF16))

    for tile in range(MT):
        i = tile_block(tile)
        i_f = i.astype(F32)
        qt = q_ref[tile * T:(tile + 1) * T, :].T
        for hd in range(N_HEADS):
            hs = slice(hd * HEAD_DIM, (hd + 1) * HEAD_DIM)
            qt_h = qt[hs, :]
            gate = jnp.where(blk_f < i_f, _dot3(km[:, hs], qt_h, _nn), NEG_INF)
            sel = _top3_mask(gate, blk_f, i_f, axis=0)
            bias = jnp.where((sel > 0.0) | (blk_iota == i), 0.0, MASKED)
            if nb < MAX_BLOCKS:
                bias = jnp.concatenate([bias, jnp.zeros((MAX_BLOCKS - nb, T), F32)], axis=0)
            qaug_scr[tile * N_HEADS + hd] = jnp.concatenate(
                [qt_h * LOG2E, bias, log2e_rows, pad_rows], axis=0).astype(BF16)
    m0, mx0 = [], []
    for c, (tile, hd) in enumerate(chains):
        s = jnp.where(causal, _nn(kgroup(tile_block(tile), hd), qaug_scr[c]), NEG_INF)
        s_scr[1, c] = s
        m0.append(jnp.max(s, axis=0, keepdims=True))
    for c, (tile, hd) in enumerate(chains):
        s = _nn(kgroup(0, hd), qaug_scr[c])
        s_scr[0, c] = s
        mx0.append(jnp.max(s, axis=0, keepdims=True))
    for c, (tile, hd) in enumerate(chains):
        acc_scr[c] = _nn(vgroup(tile_block(tile), hd), jnp.exp2(s_scr[1, c] - m0[c]).astype(BF16))

    def stage(n, cur, nxt, ms, mxs):
        ms_out, mxs_out = [], []
        for c, (tile, hd) in enumerate(chains):
            i = tile_block(tile)
            s_next = _nn(kgroup(jnp.minimum(n + 1, i), hd), qaug_scr[c])
            s_scr[nxt, c] = s_next
            mxs_out.append(jnp.max(s_next, axis=0, keepdims=True))
            cn = jnp.where(n < i, block_term(n, i, hd), NEG_INF)
            m_new, acc_scr[c] = online_update(s_scr[cur, c], mxs[c], cn, ms[c], acc_scr[c],
                                              vgroup(jnp.minimum(n, i), hd))
            ms_out.append(m_new)
        return tuple(ms_out), tuple(mxs_out)

    def body(j, carry):
        ms, mxs = stage(2 * j, 0, 1, *carry)
        return stage(2 * j + 1, 1, 0, ms, mxs)

    lax.fori_loop(0, (tile_block(MT - 1) + 1) // 2, body, (tuple(m0), tuple(mx0)))
    for tile in range(MT):
        rows = slice(tile * T, (tile + 1) * T)
        for hd in range(N_HEADS):
            acc = acc_scr[tile * N_HEADS + hd]
            outt_scr[tile, hd * HEAD_DIM:(hd + 1) * HEAD_DIM, :] = acc[0:HEAD_DIM] / acc[HEAD_DIM:HEAD_DIM + 1]
        yatt = (outt_scr[tile].T * az_ref[rows, :].astype(F32)).astype(BF16)
        y = x_ref[rows, :]
        for g, part in enumerate((yconv_ref[rows, :], ypool_ref[rows, :], yatt, ymem_ref[rows, :])):
            y = y + _nn(part, wout_ref[g * W_GROUP:(g + 1) * W_GROUP, :])
        y_ref[rows, :] = y


def _prompt_moba(q, kaug, vaug, kmean, az, x, yconv, ypool, ymem, w_out_bf):
    batch, seq, _ = q.shape
    nt = seq // ROW_TILE
    MT = MOBA_TILES
    assert nt % MT == 0
    R = MT * ROW_TILE
    row = lambda b, t: (b, t, 0)
    part = pl.BlockSpec((None, R, W_GROUP), row)
    return pl.pallas_call(
        _prompt_moba_kernel,
        grid=(batch, nt // MT),
        in_specs=[
            part,
            pl.BlockSpec((None, nt, ROW_TILE, K_AUG_LANES), lambda b, t: (b, 0, 0, 0)),
            pl.BlockSpec((None, nt, N_HEADS * V_AUG_ROWS, ROW_TILE), lambda b, t: (b, 0, 0, 0)),
            pl.BlockSpec((None, nt, W_GROUP), lambda b, t: (b, 0, 0)),
            part,
            pl.BlockSpec((None, R, D_MODEL), row),
            part, part, part,
            pl.BlockSpec((D_MODEL, D_MODEL), lambda b, t: (0, 0)),
        ],
        out_specs=pl.BlockSpec((None, R, D_MODEL), row),
        out_shape=jax.ShapeDtypeStruct((batch, seq, D_MODEL), F32),
        scratch_shapes=[
            pltpu.VMEM((MT * N_HEADS, K_GROUP_LANES, ROW_TILE), BF16),
            pltpu.VMEM((MT * N_HEADS, V_AUG_ROWS, ROW_TILE), F32),
            pltpu.VMEM((MT, W_GROUP, ROW_TILE), F32),
            pltpu.VMEM((2, MT * N_HEADS, MOBA_BLOCK, ROW_TILE), F32),
        ],
        compiler_params=pltpu.CompilerParams(dimension_semantics=("arbitrary", "arbitrary"),
                                             vmem_limit_bytes=VMEM_LIMIT),
        name="prompt_moba",
    )(q, kaug, vaug, kmean, az, x, yconv, ypool, ymem, w_out_bf)


def _out_kernel(x_ref, yconv_ref, ypool_ref, yatt_ref, ymem_ref, w_ref, y_ref):
    acc = x_ref[...]
    for g, ref in enumerate((yconv_ref, ypool_ref, yatt_ref, ymem_ref)):
        acc = acc + _nn(ref[...].astype(BF16), w_ref[g * W_GROUP:(g + 1) * W_GROUP, :])
    y_ref[...] = acc


def _out_proj(x2, yconv, ypool, yatt, ymem, w_out_bf):
    rows = x2.shape[0]
    row = lambda r: (r, 0)
    part = pl.BlockSpec((ROW_TILE, W_GROUP), row)
    return pl.pallas_call(
        _out_kernel,
        grid=(rows // ROW_TILE,),
        in_specs=[pl.BlockSpec((ROW_TILE, D_MODEL), row), part, part, part, part,
                  pl.BlockSpec((D_MODEL, D_MODEL), lambda r: (0, 0))],
        out_specs=pl.BlockSpec((ROW_TILE, D_MODEL), row),
        out_shape=jax.ShapeDtypeStruct((rows, D_MODEL), F32),
        compiler_params=pltpu.CompilerParams(dimension_semantics=("arbitrary",),
                                             vmem_limit_bytes=VMEM_LIMIT),
        name="out_proj",
    )(x2, yconv, ypool, yatt, ymem, w_out_bf)


def _sample_front_kernel(start, x_ref, gn_ref, win_ref, cw_ref, cb_ref, pw_ref, ps_ref, gq_ref, gk_ref, gmq_ref,
                         bd_ref, sconv_ref, spool_ref,
                         q_ref, k_ref, v_ref, mq_ref, az_ref, mz_ref, yconv_ref, ypool_ref, cstate_ref, pstate_ref,
                         u_scr, e_scr):
    nb, ts = sconv_ref.shape[0], u_scr.shape[1] - CONV_HALO
    rows = nb * ts
    h = _rms(x_ref[...], gn_ref[...]).astype(BF16)
    bd = bd_ref[...]

    def proj(g):
        c = SPLIT_COLUMN[g]
        return _nn(h, win_ref[:, c * W_GROUP:(c + 1) * W_GROUP])

    def to3(a):
        return a.reshape(nb, ts, W_GROUP)

    def to2(a):
        return a.reshape(rows, W_GROUP)

    u = proj(2) * proj(0)
    u_scr[:, 0:CONV_HALO - 2, :] = jnp.zeros((nb, CONV_HALO - 2, W_GROUP), F32)
    u_scr[:, CONV_HALO - 2:CONV_HALO, :] = sconv_ref[...]
    u_scr[:, CONV_HALO:CONV_HALO + ts, :] = to3(u)
    cw = cw_ref[...]
    conv3 = (cw[0:1] * u_scr[:, CONV_HALO - 2:CONV_HALO - 2 + ts, :]
             + cw[1:2] * u_scr[:, CONV_HALO - 1:CONV_HALO - 1 + ts, :]
             + cw[2:3] * u_scr[:, CONV_HALO:CONV_HALO + ts, :])
    conv = cb_ref[...] + to2(conv3)
    yconv_ref[...] = proj(1) * conv * _silu(proj(3))
    cstate_ref[...] = u_scr[:, CONV_HALO + ts - 2:CONV_HALO + ts, :]

    H = 16
    pv = proj(4)
    e_scr[:, 0:1, :] = jnp.zeros((nb, 1, W_GROUP), F32)
    e_scr[:, 1:H, :] = spool_ref[...]
    e_scr[:, H:H + ts, :] = to3(pv)
    lane, wlane = _pool_lane_consts()
    wsum = jnp.zeros((nb, ts, W_GROUP), F32)
    for j in range(max(POOL_WINDOWS)):
        shifted = e_scr[:, H - j:H - j + ts, :]
        wsum = wsum + (shifted if j < min(POOL_WINDOWS) else jnp.where(wlane > j, shifted, 0.0))
    pos = start + lax.broadcasted_iota(jnp.int32, (ts, 1), 0)
    cnt = jnp.minimum(pos + 1, wlane).astype(F32)
    pooled = (to2(wsum / cnt) - pv).astype(BF16)
    pool_out = _nn(pooled, pw_ref[...]) * ps_ref[...]
    ypool_ref[...] = pool_out * _silu(proj(5))
    pstate_ref[...] = e_scr[:, H + ts - POOL_STATE:H + ts, :]

    q_ref[...] = _head_rms(proj(6), gq_ref[...], bd) * (HEAD_DIM ** -0.5)
    k_ref[...] = _head_rms(proj(7), gk_ref[...], bd)
    v_ref[...] = proj(8)
    az_ref[...] = _silu(proj(9))
    mq_ref[...] = _head_rms(proj(10), gmq_ref[...], bd) * (HEAD_DIM ** -0.5)
    mz_ref[...] = _silu(proj(11))


def _sample_front(x2, lw, sconv, spool, start):
    nb = sconv.shape[0]
    rows = x2.shape[0]
    ts = rows // nb
    full = lambda shape: pl.BlockSpec(shape, lambda i, n=len(shape): (0,) * n)
    t2 = jax.ShapeDtypeStruct((rows, W_GROUP), F32)
    return pl.pallas_call(
        functools.partial(_sample_front_kernel, start),
        grid=(1,),
        in_specs=[
            full((rows, D_MODEL)), full((1, D_MODEL)), full((D_MODEL, N_IN_SPLITS * W_GROUP)),
            full((CONV_W, W_GROUP)), full((1, W_GROUP)), full((W_GROUP, W_GROUP)),
            full((1, W_GROUP)), full((1, W_GROUP)), full((1, W_GROUP)), full((1, W_GROUP)),
            full((W_GROUP, W_GROUP)),
            full((nb, CONV_W - 1, W_GROUP)), full((nb, POOL_STATE, W_GROUP)),
        ],
        out_specs=[full((rows, W_GROUP))] * 8 + [full((nb, CONV_W - 1, W_GROUP)), full((nb, POOL_STATE, W_GROUP))],
        out_shape=[t2] * 8 + [jax.ShapeDtypeStruct((nb, CONV_W - 1, W_GROUP), F32),
                              jax.ShapeDtypeStruct((nb, POOL_STATE, W_GROUP), F32)],
        scratch_shapes=[
            pltpu.VMEM((nb, CONV_HALO + ts, W_GROUP), F32),
            pltpu.VMEM((nb, 16 + ts, W_GROUP), F32),
        ],
        compiler_params=pltpu.CompilerParams(dimension_semantics=("arbitrary",),
                                             vmem_limit_bytes=VMEM_LIMIT),
        name="sample_front",
    )(x2, lw["g_norm"], lw["w_in"], lw["conv_w"], lw["conv_b"], lw["pool_w"], lw["pool_scale"],
      lw["g_q"], lw["g_k"], lw["g_mq"], lw["bd"], sconv, spool)


def _head_rows(a, ts):
    r = lax.broadcasted_iota(jnp.int32, (N_HEADS * ts, W_GROUP), 0) // ts
    c = lax.broadcasted_iota(jnp.int32, (N_HEADS * ts, W_GROUP), 1) // HEAD_DIM
    return jnp.where(r == c, jnp.concatenate([a] * N_HEADS, axis=0), 0.0)


def _head_diag(o, ts):
    r = lax.broadcasted_iota(jnp.int32, (N_HEADS * ts, W_GROUP), 0) // ts
    c = lax.broadcasted_iota(jnp.int32, (N_HEADS * ts, W_GROUP), 1) // HEAD_DIM
    o = jnp.where(r == c, o, 0.0)
    out = o[0:ts]
    for hd in range(1, N_HEADS):
        out = out + o[hd * ts:(hd + 1) * ts]
    return out


def _sample_moba_kernel(layer, past_len, pt_ref, q_ref, knew_ref, vnew_ref, mq_ref, az_ref, mz_ref, memkt_ref,
                        memvt_ref, ck_hbm, cv_hbm, yatt_ref, ymem_ref, buf, sem, s_scr, p_scr):
    P = PAGES_PER_STEP
    BPS = P // PAGES_PER_BLOCK
    b = pl.program_id(0)
    n_seqs = pl.num_programs(0)
    ts = q_ref.shape[0]
    R = N_HEADS * ts
    n_pages = past_len // PAGE_SIZE
    nblk = past_len // MOBA_BLOCK
    nc = n_pages // P
    n_chunks = 2 * nc

    def chunk_copies(seq, g):
        src = ck_hbm if g < nc else cv_hbm
        first = (g % nc) * P
        slot = g % CHUNK_BUFS
        return [pltpu.make_async_copy(src.at[layer, pt_ref[seq, first + j]], buf.at[slot, j], sem.at[slot])
                for j in range(P)]

    def start(seq, g):
        for j, cp in enumerate(chunk_copies(seq, g)):
            cp.start(priority=j % 2)

    def wait(seq, g):
        for cp in chunk_copies(seq, g):
            cp.wait()

    def prefetch_after(g):
        ahead = g + CHUNK_BUFS - 1
        if ahead < n_chunks:
            start(b, ahead)
        else:
            @pl.when(b + 1 < n_seqs)
            def _():
                start(b + 1, ahead - n_chunks)

    @pl.when(b == 0)
    def _():
        for g in range(CHUNK_BUFS - 1):
            start(b, g)

    qrows = _head_rows(q_ref[...], ts)
    row_t = (lax.broadcasted_iota(jnp.int32, (R, 1), 0) % ts).astype(F32)
    row_h = lax.broadcasted_iota(jnp.int32, (R, 1), 0) // ts
    slope = jnp.where(row_h == 0, _slope(0), jnp.where(row_h == 1, _slope(1),
                                                        jnp.where(row_h == 2, _slope(2), _slope(3))))

    qb = qrows.astype(BF16)
    lane8 = lax.broadcasted_iota(jnp.int32, (W_GROUP, BPS), 1)
    ksums = []
    for g in range(nc):
        prefetch_after(g)
        wait(b, g)
        sums = jnp.zeros((W_GROUP, BPS), F32)
        for j in range(0, P, PAGES_PER_BLOCK):
            ka, kb = buf[g % CHUNK_BUFS, j], buf[g % CHUNK_BUFS, j + 1]
            s_scr[g * P + j] = _nn(qb, ka.astype(BF16))
            s_scr[g * P + j + 1] = _nn(qb, kb.astype(BF16))
            tot = jnp.sum(ka + kb, axis=1, keepdims=True)
            sums = jnp.where(lane8 == j // PAGES_PER_BLOCK, tot, sums)
        ksums.append(sums)

    gate = jnp.concatenate([_dot3(qrows, ks * (1.0 / MOBA_BLOCK), _nn) for ks in ksums], axis=1)
    blk_iota = lax.broadcasted_iota(jnp.int32, (R, nblk), 1).astype(F32)
    sel = _top3_mask(gate, blk_iota, float(nblk), axis=-1)
    blk_term = jnp.where(sel > 0.0, slope * (blk_iota * MOBA_BLOCK - past_len), NEG_INF)
    col = lax.broadcasted_iota(jnp.int32, (R, PAGE_SIZE), 1).astype(F32)
    page_term = [slope * (col + float(k * PAGE_SIZE)) for k in range(PAGES_PER_BLOCK)]
    own = []
    m = jnp.full((R, 1), NEG_INF, F32)
    for j in range(ts):
        sj = jnp.sum(qrows * knew_ref[j:j + 1, :], axis=-1, keepdims=True) + slope * float(j)
        sj = jnp.where(row_t >= float(j), sj, NEG_INF)
        own.append(sj)
        m = jnp.maximum(m, sj)
    mvec = jnp.full((R, PAGE_SIZE), NEG_INF, F32)
    for n in range(nblk):
        for k in range(PAGES_PER_BLOCK):
            pg = n * PAGES_PER_BLOCK + k
            s = s_scr[pg] + page_term[k] + blk_term[:, n:n + 1]
            s_scr[pg] = s
            mvec = jnp.maximum(mvec, s)
    m = jnp.maximum(m, jnp.max(mvec, axis=-1, keepdims=True))
    p_own = [jnp.exp(sj - m) for sj in own]
    lvec = jnp.zeros((R, PAGE_SIZE), F32)
    for pg in range(n_pages):
        p = jnp.exp(s_scr[pg] - m)
        p_scr[pg] = p.astype(BF16)
        lvec = lvec + p
    l = sum(p_own) + jnp.sum(lvec, axis=-1, keepdims=True)

    acc = jnp.zeros((R, W_GROUP), F32)
    for g in range(nc, n_chunks):
        prefetch_after(g)
        wait(b, g)
        for j in range(P):
            acc = acc + _nt(p_scr[(g - nc) * P + j], buf[g % CHUNK_BUFS, j].astype(BF16))
    for j in range(ts):
        acc = acc + p_own[j] * vnew_ref[j:j + 1, :]
    yatt_ref[...] = _head_diag(acc / l, ts) * az_ref[...]

    mrows = _head_rows(mq_ref[...], ts).astype(BF16)
    s = _nn(mrows, memkt_ref[...].astype(BF16))
    p = jnp.exp(s - jnp.max(s, axis=-1, keepdims=True))
    lm = jnp.sum(p, axis=-1, keepdims=True)
    o = _nt(p.astype(BF16), memvt_ref[...].astype(BF16)) / lm
    ymem_ref[...] = _head_diag(o, ts) * mz_ref[...]


def _sample_moba(layer, page_table, q, knew, vnew, mq, az, mz, memkt, memvt, cache_kt, cache_vt):
    nb, n_pages = page_table.shape
    rows = q.shape[0]
    ts = rows // nb
    P = PAGES_PER_STEP
    assert n_pages % P == 0 and P % PAGES_PER_BLOCK == 0
    assert (2 * n_pages // P) % CHUNK_BUFS == 0
    past_len = n_pages * PAGE_SIZE
    R = N_HEADS * ts
    seq = pl.BlockSpec((ts, W_GROUP), lambda b, pt: (b, 0))
    mem = pl.BlockSpec((None, None, W_GROUP, N_MEM), lambda b, pt: (layer, b, 0, 0))
    hbm = pl.BlockSpec(memory_space=pl.ANY)
    out = jax.ShapeDtypeStruct((rows, W_GROUP), F32)
    grid_spec = pltpu.PrefetchScalarGridSpec(
        num_scalar_prefetch=1,
        grid=(nb,),
        in_specs=[seq, seq, seq, seq, seq, seq, mem, mem, hbm, hbm],
        out_specs=[seq, seq],
        scratch_shapes=[
            pltpu.VMEM((CHUNK_BUFS, P, W_GROUP, PAGE_SIZE), F32),
            pltpu.SemaphoreType.DMA((CHUNK_BUFS,)),
            pltpu.VMEM((n_pages, R, PAGE_SIZE), F32),
            pltpu.VMEM((n_pages, R, PAGE_SIZE), BF16),
        ],
    )
    return pl.pallas_call(
        functools.partial(_sample_moba_kernel, layer, past_len),
        grid_spec=grid_spec,
        out_shape=[out, out],
        compiler_params=pltpu.CompilerParams(dimension_semantics=("arbitrary",),
                                             vmem_limit_bytes=VMEM_LIMIT),
        name="sample_moba",
    )(page_table, q, knew, vnew, mq, az, mz, memkt, memvt, cache_kt, cache_vt)


def _same_head_matrix():
    r = lax.broadcasted_iota(jnp.int32, (W_GROUP, W_GROUP), 0) // HEAD_DIM
    c = lax.broadcasted_iota(jnp.int32, (W_GROUP, W_GROUP), 1) // HEAD_DIM
    return (r == c).astype(BF16)


def _pool_block_diag(pool_w_l):
    out = jnp.zeros((W_GROUP, W_GROUP), pool_w_l.dtype)
    for g in range(len(POOL_WINDOWS)):
        out = out.at[g * 64:(g + 1) * 64, g * 64:(g + 1) * 64].set(pool_w_l[g])
    return out


def _tokens_minor(a):
    lead = a.shape[:-3]
    n = len(lead)
    perm = tuple(range(n)) + (n + 1, n + 2, n)
    return a.transpose(perm).reshape(lead + (W_GROUP, a.shape[-3]))


def _tokens_major(a_t):
    lead = a_t.shape[:-2]
    n = len(lead)
    perm = tuple(range(n)) + (n + 2, n, n + 1)
    return a_t.reshape(lead + (N_HEADS, HEAD_DIM, a_t.shape[-1])).transpose(perm)


def kernel(x_prompt, x_sample, cache_k, cache_v, page_table, state_conv, state_pool, cache_mem_k, cache_mem_v,
           mem_prompt, g_norm, w_in, w_out, conv_w, conv_b, pool_w, pool_scale, g_q, g_k, g_mq, g_mk, g_mem,
           w_mem_kv):
    depth = w_in.shape[0]
    bp, seq, _ = x_prompt.shape
    bs, ts, _ = x_sample.shape
    n_pages = page_table.shape[1]
    past_len = n_pages * PAGE_SIZE
    bd = _same_head_matrix()
    tile4 = lambda g: jnp.tile(g, (1, N_HEADS))[:, None, :]

    w_in_bf = jnp.concatenate([w_in[:, :, g * W_GROUP:(g + 1) * W_GROUP].astype(BF16) for g in SPLIT_ORDER], axis=2)
    w_out_bf = w_out.astype(BF16)
    g_q_t, g_k_t, g_mq_t, g_mk_t = tile4(g_q), tile4(g_k), tile4(g_mq), tile4(g_mk)
    layer_w = []
    for l in range(depth):
        layer_w.append(dict(
            g_norm=g_norm[l][None, :], w_in=w_in_bf[l], conv_w=conv_w[l], conv_b=conv_b[l][None, :],
            pool_w=_pool_block_diag(pool_w[l]).astype(BF16), pool_scale=pool_scale[l][None, :],
            g_q=g_q_t[l], g_k=g_k_t[l], g_mq=g_mq_t[l], bd=bd))

    mkt_all, mvt_all, mk_bf, mvaug = _memkv(mem_prompt, g_mem[:, None, :], w_mem_kv.astype(BF16), g_mk_t, bd)

    cache_kt = _tokens_minor(cache_k)
    cache_vt = _tokens_minor(cache_v)
    memkt_s = _tokens_minor(cache_mem_k)
    memvt_s = _tokens_minor(cache_mem_v)

    xp = x_prompt
    xs = x_sample.reshape(bs * ts, D_MODEL)
    ks_l, vs_l, cp_l, cs_l, pp_l, ps_l = ([] for _ in range(6))
    kt, vt = None, None
    for l in range(depth):
        lw = layer_w[l]
        (q, kt, vt, kaug, vaug, kmean, yconv, ypool, ymem, az, cst, pst) = _prompt_front(
            xp, lw, mk_bf[l], mvaug[l], kt, vt)
        xp = _prompt_moba(q, kaug, vaug, kmean.reshape(bp, seq // ROW_TILE, W_GROUP), az,
                          xp, yconv, ypool, ymem, w_out_bf[l])
        cp_l.append(cst)
        pp_l.append(pst)
        (qs, ksn, vsn, mqs, azs, mzs, yconv_s, ypool_s, cst_s, pst_s) = _sample_front(
            xs, lw, state_conv[l], state_pool[l], past_len)
        yatt_s, ymem_s = _sample_moba(l, page_table, qs, ksn, vsn, mqs, azs, mzs, memkt_s, memvt_s,
                                      cache_kt, cache_vt)
        xs = _out_proj(xs, yconv_s, ypool_s, yatt_s, ymem_s, w_out_bf[l])
        ks_l.append(ksn.reshape(bs, ts, N_HEADS, HEAD_DIM))
        vs_l.append(vsn.reshape(bs, ts, N_HEADS, HEAD_DIM))
        cs_l.append(cst_s)
        ps_l.append(pst_s)

    return (xp, xs.reshape(bs, ts, D_MODEL), _tokens_major(kt), _tokens_major(vt), jnp.stack(ks_l), jnp.stack(vs_l),
            jnp.stack(cp_l), jnp.stack(cs_l), jnp.stack(pp_l), jnp.stack(ps_l),
            _tokens_major(mkt_all), _tokens_major(mvt_all))
```

```python
import functools
import struct

import jax
import jax.numpy as jnp
from jax import lax
from jax.experimental import pallas as pl
from jax.experimental.pallas import tpu as pltpu

F32 = jnp.float32
BF16 = jnp.bfloat16

D_MODEL = 1024
W_GROUP = 256
N_HEADS = 4
HEAD_DIM = 64
N_MEM = 256
CONV_W = 3
POOL_WINDOWS = (2, 4, 8, 16)
POOL_STATE = 15
MOBA_BLOCK = 256
MOBA_TOPK = 3
PAGE_SIZE = 128
N_IN_SPLITS = 12
EPS = 1e-6
NEG_INF = float("-inf")

ROW_TILE = 256
FRONT_TILES = 4
MOBA_TILES = 2
POOL_HALO = 32
CONV_HALO = 8
PAGES_PER_STEP = 16
CHUNK_BUFS = 4
PAGES_PER_BLOCK = MOBA_BLOCK // PAGE_SIZE
VMEM_LIMIT = 56 * 1024 * 1024

SPLIT_ORDER = (10, 7, 6, 8, 11, 9, 4, 5, 2, 0, 1, 3)
SPLIT_COLUMN = {g: c for c, g in enumerate(SPLIT_ORDER)}

MAX_BLOCKS = 32
AUG_LANES = 64
K_GROUP_LANES = HEAD_DIM + AUG_LANES
K_AUG_LANES = N_HEADS * K_GROUP_LANES
V_AUG_ROWS = 80
MASKED = -1e30


def _bf16_round(x):
    bits = struct.unpack("<I", struct.pack("<f", x))[0]
    bits = (bits + 0x7FFF + ((bits >> 16) & 1)) & 0xFFFF0000
    return struct.unpack("<f", struct.pack("<I", bits))[0]


LOG2E = 1.4426950408889634
LOG2E_HI = _bf16_round(LOG2E)
LOG2E_LO = LOG2E - LOG2E_HI


def _slope(h):
    return 2.0 ** (-8.0 * (h + 1) / N_HEADS)


def _nt(a, b):
    return lax.dot_general(a, b, (((1,), (1,)), ((), ())), preferred_element_type=F32)


def _split_bf16(x):
    hi = x.astype(BF16)
    lo = (x - hi.astype(F32)).astype(BF16)
    return hi, lo


def _dot3(a, b, dot):
    ah, al = _split_bf16(a)
    bh, bl = _split_bf16(b)
    return dot(ah, bh) + dot(ah, bl) + dot(al, bh)


def _nn(a, b):
    return jnp.dot(a, b, preferred_element_type=F32)


def _rms(x, g):
    ms = jnp.mean(x * x, axis=-1, keepdims=True)
    return x * lax.rsqrt(ms + EPS) * g


def _head_rstd(x, bd):
    hi, lo = _split_bf16(x * x)
    ssq = _nn(hi, bd) + _nn(lo, bd)
    return lax.rsqrt(ssq * (1.0 / HEAD_DIM) + EPS)


def _head_rms(x, g, bd):
    return x * _head_rstd(x, bd) * g


def _silu(z):
    return z / (1.0 + jnp.exp(-z))


def _top3_mask(gate, blk_iota, limit, axis):
    sel = jnp.zeros(gate.shape, F32)
    g = gate
    for _ in range(MOBA_TOPK):
        m = jnp.max(g, axis=axis, keepdims=True)
        idx = jnp.min(jnp.where(g == m, blk_iota, 1e9), axis=axis, keepdims=True)
        pick = blk_iota == idx
        sel = jnp.where(pick & (idx < limit), 1.0, sel)
        g = jnp.where(pick, NEG_INF, g)
    return sel


def _pool_lane_consts():
    lane = lax.broadcasted_iota(jnp.int32, (1, W_GROUP), 1)
    w = jnp.where(lane < 64, 2, jnp.where(lane < 128, 4, jnp.where(lane < 192, 8, 16)))
    return lane, w


def _value_groups(vt, n_tok):
    ones_row = (lax.broadcasted_iota(jnp.int32, (V_AUG_ROWS - HEAD_DIM, n_tok), 0) == 0).astype(F32)
    parts = []
    for hd in range(N_HEADS):
        parts += [vt[hd * HEAD_DIM:(hd + 1) * HEAD_DIM, :], ones_row]
    return jnp.concatenate(parts, axis=0).astype(BF16)


def _memkv_kernel(mem_ref, g_ref, w_ref, gmk_ref, bd_ref, mkt_ref, mvt_ref, mkb_ref, mvaug_ref):
    h = _rms(mem_ref[...], g_ref[...]).astype(BF16)
    kv = _nn(h, w_ref[...])
    mk = _head_rms(kv[:, :W_GROUP], gmk_ref[...], bd_ref[...])
    mvt = kv[:, W_GROUP:].T
    mkt_ref[...] = mk.T
    mvt_ref[...] = mvt
    mkb_ref[...] = mk.astype(BF16)
    mvaug_ref[...] = _value_groups(mvt, N_MEM)


def _memkv(mem_prompt, g_mem, w_mem_kv_bf, g_mk_t, bd):
    depth = g_mem.shape[0]
    batch = mem_prompt.shape[0]
    out = lambda dt: jax.ShapeDtypeStruct((depth, batch, N_MEM, W_GROUP), dt)
    ospec = pl.BlockSpec((None, None, N_MEM, W_GROUP), lambda l, b: (l, b, 0, 0))
    vaug_rows = N_HEADS * V_AUG_ROWS
    return pl.pallas_call(
        _memkv_kernel,
        grid=(depth, batch),
        in_specs=[
            pl.BlockSpec((None, N_MEM, D_MODEL), lambda l, b: (b, 0, 0)),
            pl.BlockSpec((None, 1, D_MODEL), lambda l, b: (l, 0, 0)),
            pl.BlockSpec((None, D_MODEL, 2 * W_GROUP), lambda l, b: (l, 0, 0)),
            pl.BlockSpec((None, 1, W_GROUP), lambda l, b: (l, 0, 0)),
            pl.BlockSpec((W_GROUP, W_GROUP), lambda l, b: (0, 0)),
        ],
        out_specs=[ospec, ospec, ospec,
                   pl.BlockSpec((None, None, vaug_rows, N_MEM), lambda l, b: (l, b, 0, 0))],
        out_shape=[out(F32), out(F32), out(BF16),
                   jax.ShapeDtypeStruct((depth, batch, vaug_rows, N_MEM), BF16)],
        compiler_params=pltpu.CompilerParams(dimension_semantics=("arbitrary", "arbitrary")),
        name="memkv",
    )(mem_prompt, g_mem, w_mem_kv_bf, g_mk_t, bd)


def _prompt_front_kernel(n_prev, x_ref, gn_ref, win_ref, cw_ref, cb_ref, pw_ref, ps_ref, gq_ref, gk_ref, gmq_ref,
                         bd_ref, memk_ref, memvaug_ref, *refs):
    if n_prev:
        ktprev_ref, vtprev_ref, *refs = refs
    (q_ref, kt_ref, vt_ref, kaug_ref, vaug_ref, kmean_ref, yconv_ref, ypool_ref, ymem_ref, az_ref,
     cstate_ref, pstate_ref, u_scr, e_scr, s2_scr, s4_scr, s8_scr, proj_scr, ot_scr, ms_scr, rstd_scr) = refs
    if n_prev:
        kt_ref[0:n_prev] = ktprev_ref[...]
        vt_ref[0:n_prev] = vtprev_ref[...]
    step = pl.program_id(1)
    T = ROW_TILE
    H = POOL_HALO
    NT = FRONT_TILES

    @pl.when(step == 0)
    def _():
        u_scr[0:CONV_HALO, :] = jnp.zeros((CONV_HALO, W_GROUP), F32)
        e_scr[0:H, :] = jnp.zeros((H, W_GROUP), F32)

    bd = bd_ref[...]
    cw = cw_ref[...]
    lane, wlane = _pool_lane_consts()
    lane64 = lax.broadcasted_iota(jnp.int32, (T, AUG_LANES), 1)
    key_idx = lax.broadcasted_iota(jnp.int32, (T, AUG_LANES), 0).astype(F32)
    slope_lanes = (lane64 == MAX_BLOCKS) | (lane64 == MAX_BLOCKS + 1)

    for sub in range(NT):
        t = step * NT + sub
        rows = slice(sub * T, (sub + 1) * T)

        h = _rms(x_ref[rows, :], gn_ref[...]).astype(BF16)
        proj_scr[sub] = _nn(h, win_ref[...])

        def proj(g, sub=sub):
            c = SPLIT_COLUMN[g]
            return proj_scr[sub, :, c * W_GROUP:(c + 1) * W_GROUP]

        for j, g in enumerate((6, 7, 10)):
            rstd_scr[sub, j] = _head_rstd(proj(g), bd)

        ub = CONV_HALO + sub * T
        u = proj(2) * proj(0)
        u_scr[ub:ub + T, :] = u
        conv = (cb_ref[...] + cw[0:1] * u_scr[ub - 2:ub - 2 + T, :]
                + cw[1:2] * u_scr[ub - 1:ub - 1 + T, :] + cw[2:3] * u)
        yconv_ref[rows, :] = (proj(1) * conv * _silu(proj(3))).astype(BF16)

        eb = sub * T
        pv = proj(4)
        e_scr[eb + H:eb + H + T, :] = pv
        s2_scr[sub, 8:H + T, :] = e_scr[eb + 8:eb + H + T, :] + e_scr[eb + 7:eb + H + T - 1, :]
        s4_scr[sub, 16:H + T, :] = s2_scr[sub, 16:H + T, :] + s2_scr[sub, 14:H + T - 2, :]
        s8_scr[sub, 24:H + T, :] = s4_scr[sub, 24:H + T, :] + s4_scr[sub, 20:H + T - 4, :]
        s16 = s8_scr[sub, H:H + T, :] + s8_scr[sub, H - 8:H + T - 8, :]
        wsum = jnp.where(lane < 64, s2_scr[sub, H:H + T, :],
                         jnp.where(lane < 128, s4_scr[sub, H:H + T, :],
                                   jnp.where(lane < 192, s8_scr[sub, H:H + T, :], s16)))
        pos = t * T + lax.broadcasted_iota(jnp.int32, (T, 1), 0)
        cnt = jnp.minimum(pos + 1, wlane).astype(F32)
        pooled = (wsum / cnt - pv).astype(BF16)
        pool_out = _nn(pooled, pw_ref[...]) * ps_ref[...]
        ypool_ref[rows, :] = (pool_out * _silu(proj(5))).astype(BF16)

        q_ref[rows, :] = proj(6) * rstd_scr[sub, 0] * gq_ref[...] * (HEAD_DIM ** -0.5)
        k = proj(7) * rstd_scr[sub, 1] * gk_ref[...]
        v = proj(8)
        vt = v.T
        kt_ref[n_prev, :, rows] = k.T
        vt_ref[n_prev, :, rows] = vt
        kmean_ref[sub] = jnp.mean(k, axis=0, keepdims=True)
        onehot = (lane64 == t).astype(F32)
        kparts = []
        for hd in range(N_HEADS):
            kparts += [k[:, hd * HEAD_DIM:(hd + 1) * HEAD_DIM],
                       jnp.where(slope_lanes, _slope(hd) * key_idx, onehot)]
        kaug_ref[sub] = jnp.concatenate(kparts, axis=1).astype(BF16)
        vaug_ref[sub] = _value_groups(vt, T)
        az_ref[rows, :] = _silu(proj(9)).astype(BF16)

        mqt = (proj(10) * rstd_scr[sub, 2] * gmq_ref[...] * (HEAD_DIM ** -0.5 * LOG2E)).T.astype(BF16)
        mmax = []
        for hd in range(N_HEADS):
            s = _nn(memk_ref[:, hd * HEAD_DIM:(hd + 1) * HEAD_DIM], mqt[hd * HEAD_DIM:(hd + 1) * HEAD_DIM, :])
            ms_scr[sub, hd] = s
            mmax.append(jnp.max(s, axis=0, keepdims=True))
        for hd in range(N_HEADS):
            hs = slice(hd * HEAD_DIM, (hd + 1) * HEAD_DIM)
            p = jnp.exp2(ms_scr[sub, hd] - mmax[hd])
            acc = _nn(memvaug_ref[hd * V_AUG_ROWS:(hd + 1) * V_AUG_ROWS, :], p.astype(BF16))
            ot_scr[sub, hs, :] = acc[0:HEAD_DIM] / acc[HEAD_DIM:HEAD_DIM + 1]
        ymem_ref[rows, :] = (ot_scr[sub].T * _silu(proj(11))).astype(BF16)

    last = NT * T
    cstate_ref[...] = u_scr[CONV_HALO + last - 2:CONV_HALO + last, :]
    pstate_ref[...] = e_scr[H + last - POOL_STATE:H + last, :]
    u_scr[0:CONV_HALO, :] = u_scr[last:last + CONV_HALO, :]
    e_scr[0:H, :] = e_scr[last:last + H, :]


def _prompt_front(x, lw, memk_bf, memvaug, kt_prev, vt_prev):
    batch, seq, _ = x.shape
    nt = seq // ROW_TILE
    NT = FRONT_TILES
    R = NT * ROW_TILE
    assert nt <= MAX_BLOCKS and nt % NT == 0
    n_prev = 0 if kt_prev is None else kt_prev.shape[0]
    row = lambda b, t: (b, t, 0)
    const2 = lambda b, t: (0, 0)
    tile = lambda dt: jax.ShapeDtypeStruct((batch, seq, W_GROUP), dt)
    tile_t = jax.ShapeDtypeStruct((n_prev + 1, batch, W_GROUP, seq), F32)
    tile_spec = pl.BlockSpec((None, R, W_GROUP), row)
    planes_spec = lambda n: pl.BlockSpec((n, None, W_GROUP, R), lambda b, t: (0, b, 0, t))
    tile_t_spec = planes_spec(n_prev + 1)
    kaug_spec = pl.BlockSpec((None, NT, ROW_TILE, K_AUG_LANES), lambda b, t: (b, t, 0, 0))
    vaug_spec = pl.BlockSpec((None, NT, N_HEADS * V_AUG_ROWS, ROW_TILE), lambda b, t: (b, t, 0, 0))
    vec = pl.BlockSpec((1, W_GROUP), const2)
    prev_specs = [planes_spec(n_prev)] * 2 if n_prev else []
    prev_args = (kt_prev, vt_prev) if n_prev else ()
    return pl.pallas_call(
        functools.partial(_prompt_front_kernel, n_prev),
        grid=(batch, nt // NT),
        in_specs=[
            pl.BlockSpec((None, R, D_MODEL), row),
            pl.BlockSpec((1, D_MODEL), const2),
            pl.BlockSpec((D_MODEL, N_IN_SPLITS * W_GROUP), const2),
            pl.BlockSpec((CONV_W, W_GROUP), const2),
            vec,
            pl.BlockSpec((W_GROUP, W_GROUP), const2),
            vec, vec, vec, vec,
            pl.BlockSpec((W_GROUP, W_GROUP), const2),
            pl.BlockSpec((None, N_MEM, W_GROUP), lambda b, t: (b, 0, 0)),
            pl.BlockSpec((None, N_HEADS * V_AUG_ROWS, N_MEM), lambda b, t: (b, 0, 0)),
        ] + prev_specs,
        out_specs=[
            tile_spec, tile_t_spec, tile_t_spec, kaug_spec, vaug_spec,
            pl.BlockSpec((None, NT, 1, W_GROUP), lambda b, t: (b, t, 0, 0)),
            tile_spec, tile_spec, tile_spec, tile_spec,
            pl.BlockSpec((None, CONV_W - 1, W_GROUP), lambda b, t: (b, 0, 0)),
            pl.BlockSpec((None, POOL_STATE, W_GROUP), lambda b, t: (b, 0, 0)),
        ],
        out_shape=[
            tile(F32), tile_t, tile_t,
            jax.ShapeDtypeStruct((batch, nt, ROW_TILE, K_AUG_LANES), BF16),
            jax.ShapeDtypeStruct((batch, nt, N_HEADS * V_AUG_ROWS, ROW_TILE), BF16),
            jax.ShapeDtypeStruct((batch, nt, 1, W_GROUP), F32),
            tile(BF16), tile(BF16), tile(BF16), tile(BF16),
            jax.ShapeDtypeStruct((batch, CONV_W - 1, W_GROUP), F32),
            jax.ShapeDtypeStruct((batch, POOL_STATE, W_GROUP), F32),
        ],
        scratch_shapes=[
            pltpu.VMEM((CONV_HALO + R, W_GROUP), F32),
            pltpu.VMEM((POOL_HALO + R, W_GROUP), F32),
            pltpu.VMEM((NT, POOL_HALO + ROW_TILE, W_GROUP), F32),
            pltpu.VMEM((NT, POOL_HALO + ROW_TILE, W_GROUP), F32),
            pltpu.VMEM((NT, POOL_HALO + ROW_TILE, W_GROUP), F32),
            pltpu.VMEM((NT, ROW_TILE, N_IN_SPLITS * W_GROUP), F32),
            pltpu.VMEM((NT, W_GROUP, ROW_TILE), F32),
            pltpu.VMEM((NT, N_HEADS, N_MEM, ROW_TILE), F32),
            pltpu.VMEM((NT, 3, ROW_TILE, W_GROUP), F32),
        ],
        compiler_params=pltpu.CompilerParams(dimension_semantics=("arbitrary", "arbitrary"),
                                             vmem_limit_bytes=VMEM_LIMIT),
        name="prompt_front",
    )(x, lw["g_norm"], lw["w_in"], lw["conv_w"], lw["conv_b"], lw["pool_w"], lw["pool_scale"],
      lw["g_q"], lw["g_k"], lw["g_mq"], lw["bd"], memk_bf, memvaug, *prev_args)


def _prompt_moba_kernel(q_ref, kaug_ref, vaug_ref, kmean_ref, az_ref, x_ref, yconv_ref, ypool_ref, ymem_ref, wout_ref,
                        y_ref, qaug_scr, acc_scr, outt_scr, s_scr):
    step = pl.program_id(1)
    T = ROW_TILE
    MT = MOBA_TILES
    nb = kmean_ref.shape[0]
    km = kmean_ref[...]
    blk_iota = lax.broadcasted_iota(jnp.int32, (nb, T), 0)
    blk_f = blk_iota.astype(F32)
    row8 = lax.broadcasted_iota(jnp.int32, (8, T), 0)
    log2e_rows = jnp.where(row8 == 0, LOG2E_HI, jnp.where(row8 == 1, LOG2E_LO, 0.0))
    pad_rows = jnp.zeros((K_GROUP_LANES - HEAD_DIM - MAX_BLOCKS - 8, T), F32)
    causal = lax.broadcasted_iota(jnp.int32, (T, T), 0) <= lax.broadcasted_iota(jnp.int32, (T, T), 1)
    chains = [(tile, hd) for tile in range(MT) for hd in range(N_HEADS)]

    def tile_block(tile):
        return step * MT + tile

    def kgroup(n, hd):
        return kaug_ref[n, :, hd * K_GROUP_LANES:(hd + 1) * K_GROUP_LANES]

    def vgroup(n, hd):
        return vaug_ref[n, hd * V_AUG_ROWS:(hd + 1) * V_AUG_ROWS, :]

    def block_term(n, i, hd):
        return (n - i).astype(F32) * (_slope(hd) * MOBA_BLOCK * LOG2E)

    def online_update(s, mx, cn, m, acc, vg):
        m_new = jnp.maximum(m, mx + cn)
        p = jnp.exp2(s - (m_new - cn))
        return m_new, jnp.exp2(m - m_new) * acc + _nn(vg, p.astype(BF16))

    for tile in range(MT):
        i = tile_block(tile)
        i_f = i.astype(F32)
        qt = q_ref[tile * T:(tile + 1) * T, :].T
        for hd in range(N_HEADS):
            hs = slice(hd * HEAD_DIM, (hd + 1) * HEAD_DIM)
            qt_h = qt[hs, :]
            gate = jnp.where(blk_f < i_f, _dot3(km[:, hs], qt_h, _nn), NEG_INF)
            sel = _top3_mask(gate, blk_f, i_f, axis=0)
            bias = jnp.where((sel > 0.0) | (blk_iota == i), 0.0, MASKED)
            if nb < MAX_BLOCKS:
                bias = jnp.concatenate([bias, jnp.zeros((MAX_BLOCKS - nb, T), F32)], axis=0)
            qaug_scr[tile * N_HEADS + hd] = jnp.concatenate(
                [qt_h * LOG2E, bias, log2e_rows, pad_rows], axis=0).astype(BF16)
    m0, mx0 = [], []
    for c, (tile, hd) in enumerate(chains):
        s = jnp.where(causal, _nn(kgroup(tile_block(tile), hd), qaug_scr[c]), NEG_INF)
        s_scr[1, c] = s
        m0.append(jnp.max(s, axis=0, keepdims=True))
    for c, (tile, hd) in enumerate(chains):
        s = _nn(kgroup(0, hd), qaug_scr[c])
        s_scr[0, c] = s
        mx0.append(jnp.max(s, axis=0, keepdims=True))
    for c, (tile, hd) in enumerate(chains):
        acc_scr[c] = _nn(vgroup(tile_block(tile), hd), jnp.exp2(s_scr[1, c] - m0[c]).astype(BF16))

    def stage(n, cur, nxt, ms, mxs):
        ms_out, mxs_out = [], []
        for c, (tile, hd) in enumerate(chains):
            i = tile_block(tile)
            s_next = _nn(kgroup(jnp.minimum(n + 1, i), hd), qaug_scr[c])
            s_scr[nxt, c] = s_next
            mxs_out.append(jnp.max(s_next, axis=0, keepdims=True))
            cn = jnp.where(n < i, block_term(n, i, hd), NEG_INF)
            m_new, acc_scr[c] = online_update(s_scr[cur, c], mxs[c], cn, ms[c], acc_scr[c],
                                              vgroup(jnp.minimum(n, i), hd))
            ms_out.append(m_new)
        return tuple(ms_out), tuple(mxs_out)

    def body(j, carry):
        ms, mxs = stage(2 * j, 0, 1, *carry)
        return stage(2 * j + 1, 1, 0, ms, mxs)

    lax.fori_loop(0, (tile_block(MT - 1) + 1) // 2, body, (tuple(m0), tuple(mx0)))
    for tile in range(MT):
        rows = slice(tile * T, (tile + 1) * T)
        for hd in range(N_HEADS):
            acc = acc_scr[tile * N_HEADS + hd]
            outt_scr[tile, hd * HEAD_DIM:(hd + 1) * HEAD_DIM, :] = acc[0:HEAD_DIM] / acc[HEAD_DIM:HEAD_DIM + 1]
        yatt = (outt_scr[tile].T * az_ref[rows, :].astype(F32)).astype(BF16)
        y = x_ref[rows, :]
        for g, part in enumerate((yconv_ref[rows, :], ypool_ref[rows, :], yatt, ymem_ref[rows, :])):
            y = y + _nn(part, wout_ref[g * W_GROUP:(g + 1) * W_GROUP, :])
        y_ref[rows, :] = y


def _prompt_moba(q, kaug, vaug, kmean, az, x, yconv, ypool, ymem, w_out_bf):
    batch, seq, _ = q.shape
    nt = seq // ROW_TILE
    MT = MOBA_TILES
    assert nt % MT == 0
    R = MT * ROW_TILE
    row = lambda b, t: (b, t, 0)
    part = pl.BlockSpec((None, R, W_GROUP), row)
    return pl.pallas_call(
        _prompt_moba_kernel,
        grid=(batch, nt // MT),
        in_specs=[
            part,
            pl.BlockSpec((None, nt, ROW_TILE, K_AUG_LANES), lambda b, t: (b, 0, 0, 0)),
            pl.BlockSpec((None, nt, N_HEADS * V_AUG_ROWS, ROW_TILE), lambda b, t: (b, 0, 0, 0)),
            pl.BlockSpec((None, nt, W_GROUP), lambda b, t: (b, 0, 0)),
            part,
            pl.BlockSpec((None, R, D_MODEL), row),
            part, part, part,
            pl.BlockSpec((D_MODEL, D_MODEL), lambda b, t: (0, 0)),
        ],
        out_specs=pl.BlockSpec((None, R, D_MODEL), row),
        out_shape=jax.ShapeDtypeStruct((batch, seq, D_MODEL), F32),
        scratch_shapes=[
            pltpu.VMEM((MT * N_HEADS, K_GROUP_LANES, ROW_TILE), BF16),
            pltpu.VMEM((MT * N_HEADS, V_AUG_ROWS, ROW_TILE), F32),
            pltpu.VMEM((MT, W_GROUP, ROW_TILE), F32),
            pltpu.VMEM((2, MT * N_HEADS, MOBA_BLOCK, ROW_TILE), F32),
        ],
        compiler_params=pltpu.CompilerParams(dimension_semantics=("arbitrary", "arbitrary"),
                                             vmem_limit_bytes=VMEM_LIMIT),
        name="prompt_moba",
    )(q, kaug, vaug, kmean, az, x, yconv, ypool, ymem, w_out_bf)


def _out_kernel(x_ref, yconv_ref, ypool_ref, yatt_ref, ymem_ref, w_ref, y_ref):
    acc = x_ref[...]
    for g, ref in enumerate((yconv_ref, ypool_ref, yatt_ref, ymem_ref)):
        acc = acc + _nn(ref[...].astype(BF16), w_ref[g * W_GROUP:(g + 1) * W_GROUP, :])
    y_ref[...] = acc


def _out_proj(x2, yconv, ypool, yatt, ymem, w_out_bf):
    rows = x2.shape[0]
    row = lambda r: (r, 0)
    part = pl.BlockSpec((ROW_TILE, W_GROUP), row)
    return pl.pallas_call(
        _out_kernel,
        grid=(rows // ROW_TILE,),
        in_specs=[pl.BlockSpec((ROW_TILE, D_MODEL), row), part, part, part, part,
                  pl.BlockSpec((D_MODEL, D_MODEL), lambda r: (0, 0))],
        out_specs=pl.BlockSpec((ROW_TILE, D_MODEL), row),
        out_shape=jax.ShapeDtypeStruct((rows, D_MODEL), F32),
        compiler_params=pltpu.CompilerParams(dimension_semantics=("arbitrary",),
                                             vmem_limit_bytes=VMEM_LIMIT),
        name="out_proj",
    )(x2, yconv, ypool, yatt, ymem, w_out_bf)


def _sample_front_kernel(start, x_ref, gn_ref, win_ref, cw_ref, cb_ref, pw_ref, ps_ref, gq_ref, gk_ref, gmq_ref,
                         bd_ref, sconv_ref, spool_ref,
                         q_ref, k_ref, v_ref, mq_ref, az_ref, mz_ref, yconv_ref, ypool_ref, cstate_ref, pstate_ref,
                         u_scr, e_scr):
    nb, ts = sconv_ref.shape[0], u_scr.shape[1] - CONV_HALO
    rows = nb * ts
    h = _rms(x_ref[...], gn_ref[...]).astype(BF16)
    bd = bd_ref[...]

    def proj(g):
        c = SPLIT_COLUMN[g]
        return _nn(h, win_ref[:, c * W_GROUP:(c + 1) * W_GROUP])

    def to3(a):
        return a.reshape(nb, ts, W_GROUP)

    def to2(a):
        return a.reshape(rows, W_GROUP)

    u = proj(2) * proj(0)
    u_scr[:, 0:CONV_HALO - 2, :] = jnp.zeros((nb, CONV_HALO - 2, W_GROUP), F32)
    u_scr[:, CONV_HALO - 2:CONV_HALO, :] = sconv_ref[...]
    u_scr[:, CONV_HALO:CONV_HALO + ts, :] = to3(u)
    cw = cw_ref[...]
    conv3 = (cw[0:1] * u_scr[:, CONV_HALO - 2:CONV_HALO - 2 + ts, :]
             + cw[1:2] * u_scr[:, CONV_HALO - 1:CONV_HALO - 1 + ts, :]
             + cw[2:3] * u_scr[:, CONV_HALO:CONV_HALO + ts, :])
    conv = cb_ref[...] + to2(conv3)
    yconv_ref[...] = proj(1) * conv * _silu(proj(3))
    cstate_ref[...] = u_scr[:, CONV_HALO + ts - 2:CONV_HALO + ts, :]

    H = 16
    pv = proj(4)
    e_scr[:, 0:1, :] = jnp.zeros((nb, 1, W_GROUP), F32)
    e_scr[:, 1:H, :] = spool_ref[...]
    e_scr[:, H:H + ts, :] = to3(pv)
    lane, wlane = _pool_lane_consts()
    wsum = jnp.zeros((nb, ts, W_GROUP), F32)
    for j in range(max(POOL_WINDOWS)):
        shifted = e_scr[:, H - j:H - j + ts, :]
        wsum = wsum + (shifted if j < min(POOL_WINDOWS) else jnp.where(wlane > j, shifted, 0.0))
    pos = start + lax.broadcasted_iota(jnp.int32, (ts, 1), 0)
    cnt = jnp.minimum(pos + 1, wlane).astype(F32)
    pooled = (to2(wsum / cnt) - pv).astype(BF16)
    pool_out = _nn(pooled, pw_ref[...]) * ps_ref[...]
    ypool_ref[...] = pool_out * _silu(proj(5))
    pstate_ref[...] = e_scr[:, H + ts - POOL_STATE:H + ts, :]

    q_ref[...] = _head_rms(proj(6), gq_ref[...], bd) * (HEAD_DIM ** -0.5)
    k_ref[...] = _head_rms(proj(7), gk_ref[...], bd)
    v_ref[...] = proj(8)
    az_ref[...] = _silu(proj(9))
    mq_ref[...] = _head_rms(proj(10), gmq_ref[...], bd) * (HEAD_DIM ** -0.5)
    mz_ref[...] = _silu(proj(11))


def _sample_front(x2, lw, sconv, spool, start):
    nb = sconv.shape[0]
    rows = x2.shape[0]
    ts = rows // nb
    full = lambda shape: pl.BlockSpec(shape, lambda i, n=len(shape): (0,) * n)
    t2 = jax.ShapeDtypeStruct((rows, W_GROUP), F32)
    return pl.pallas_call(
        functools.partial(_sample_front_kernel, start),
        grid=(1,),
        in_specs=[
            full((rows, D_MODEL)), full((1, D_MODEL)), full((D_MODEL, N_IN_SPLITS * W_GROUP)),
            full((CONV_W, W_GROUP)), full((1, W_GROUP)), full((W_GROUP, W_GROUP)),
            full((1, W_GROUP)), full((1, W_GROUP)), full((1, W_GROUP)), full((1, W_GROUP)),
            full((W_GROUP, W_GROUP)),
            full((nb, CONV_W - 1, W_GROUP)), full((nb, POOL_STATE, W_GROUP)),
        ],
        out_specs=[full((rows, W_GROUP))] * 8 + [full((nb, CONV_W - 1, W_GROUP)), full((nb, POOL_STATE, W_GROUP))],
        out_shape=[t2] * 8 + [jax.ShapeDtypeStruct((nb, CONV_W - 1, W_GROUP), F32),
                              jax.ShapeDtypeStruct((nb, POOL_STATE, W_GROUP), F32)],
        scratch_shapes=[
            pltpu.VMEM((nb, CONV_HALO + ts, W_GROUP), F32),
            pltpu.VMEM((nb, 16 + ts, W_GROUP), F32),
        ],
        compiler_params=pltpu.CompilerParams(dimension_semantics=("arbitrary",),
                                             vmem_limit_bytes=VMEM_LIMIT),
        name="sample_front",
    )(x2, lw["g_norm"], lw["w_in"], lw["conv_w"], lw["conv_b"], lw["pool_w"], lw["pool_scale"],
      lw["g_q"], lw["g_k"], lw["g_mq"], lw["bd"], sconv, spool)


def _head_rows(a, ts):
    r = lax.broadcasted_iota(jnp.int32, (N_HEADS * ts, W_GROUP), 0) // ts
    c = lax.broadcasted_iota(jnp.int32, (N_HEADS * ts, W_GROUP), 1) // HEAD_DIM
    return jnp.where(r == c, jnp.concatenate([a] * N_HEADS, axis=0), 0.0)


def _head_diag(o, ts):
    r = lax.broadcasted_iota(jnp.int32, (N_HEADS * ts, W_GROUP), 0) // ts
    c = lax.broadcasted_iota(jnp.int32, (N_HEADS * ts, W_GROUP), 1) // HEAD_DIM
    o = jnp.where(r == c, o, 0.0)
    out = o[0:ts]
    for hd in range(1, N_HEADS):
        out = out + o[hd * ts:(hd + 1) * ts]
    return out


def _sample_moba_kernel(layer, past_len, pt_ref, q_ref, knew_ref, vnew_ref, mq_ref, az_ref, mz_ref, memkt_ref,
                        memvt_ref, ck_hbm, cv_hbm, yatt_ref, ymem_ref, buf, sem, s_scr, p_scr):
    P = PAGES_PER_STEP
    BPS = P // PAGES_PER_BLOCK
    b = pl.program_id(0)
    n_seqs = pl.num_programs(0)
    ts = q_ref.shape[0]
    R = N_HEADS * ts
    n_pages = past_len // PAGE_SIZE
    nblk = past_len // MOBA_BLOCK
    nc = n_pages // P
    n_chunks = 2 * nc

    def chunk_copies(seq, g):
        src = ck_hbm if g < nc else cv_hbm
        first = (g % nc) * P
        slot = g % CHUNK_BUFS
        return [pltpu.make_async_copy(src.at[layer, pt_ref[seq, first + j]], buf.at[slot, j], sem.at[slot])
                for j in range(P)]

    def start(seq, g):
        for j, cp in enumerate(chunk_copies(seq, g)):
            cp.start(priority=j % 2)

    def wait(seq, g):
        for cp in chunk_copies(seq, g):
            cp.wait()

    def prefetch_after(g):
        ahead = g + CHUNK_BUFS - 1
        if ahead < n_chunks:
            start(b, ahead)
        else:
            @pl.when(b + 1 < n_seqs)
            def _():
                start(b + 1, ahead - n_chunks)

    @pl.when(b == 0)
    def _():
        for g in range(CHUNK_BUFS - 1):
            start(b, g)

    qrows = _head_rows(q_ref[...], ts)
    row_t = (lax.broadcasted_iota(jnp.int32, (R, 1), 0) % ts).astype(F32)
    row_h = lax.broadcasted_iota(jnp.int32, (R, 1), 0) // ts
    slope = jnp.where(row_h == 0, _slope(0), jnp.where(row_h == 1, _slope(1),
                                                        jnp.where(row_h == 2, _slope(2), _slope(3))))

    qb = qrows.astype(BF16)
    lane8 = lax.broadcasted_iota(jnp.int32, (W_GROUP, BPS), 1)
    ksums = []
    def block_of(g, i):
        return jnp.concatenate([buf[g % CHUNK_BUFS, i * PAGES_PER_BLOCK + k] for k in range(PAGES_PER_BLOCK)], axis=1)

    for g in range(nc):
        prefetch_after(g)
        wait(b, g)
        sums = jnp.zeros((W_GROUP, BPS), F32)
        for i in range(BPS):
            kt_blk = block_of(g, i)
            s_scr[g * BPS + i] = _nn(qb, kt_blk.astype(BF16))
            tot = jnp.sum(kt_blk, axis=1, keepdims=True)
            sums = jnp.where(lane8 == i, tot, sums)
        ksums.append(sums)

    gate = jnp.concatenate([_dot3(qrows, ks * (1.0 / MOBA_BLOCK), _nn) for ks in ksums], axis=1)
    blk_iota = lax.broadcasted_iota(jnp.int32, (R, nblk), 1).astype(F32)
    sel = _top3_mask(gate, blk_iota, float(nblk), axis=-1)
    blk_term = jnp.where(sel > 0.0, slope * (blk_iota * MOBA_BLOCK - past_len), NEG_INF)
    key_term = slope * lax.broadcasted_iota(jnp.int32, (R, MOBA_BLOCK), 1).astype(F32)
    own = []
    m = jnp.full((R, 1), NEG_INF, F32)
    for j in range(ts):
        sj = jnp.sum(qrows * knew_ref[j:j + 1, :], axis=-1, keepdims=True) + slope * float(j)
        sj = jnp.where(row_t >= float(j), sj, NEG_INF)
        own.append(sj)
        m = jnp.maximum(m, sj)
    mvec = jnp.full((R, MOBA_BLOCK), NEG_INF, F32)
    for n in range(nblk):
        s = s_scr[n] + key_term + blk_term[:, n:n + 1]
        s_scr[n] = s
        mvec = jnp.maximum(mvec, s)
    m = jnp.maximum(m, jnp.max(mvec, axis=-1, keepdims=True))
    p_own = [jnp.exp(sj - m) for sj in own]
    lvec = jnp.zeros((R, MOBA_BLOCK), F32)
    for n in range(nblk):
        p = jnp.exp(s_scr[n] - m)
        p_scr[n] = p.astype(BF16)
        lvec = lvec + p
    l = sum(p_own) + jnp.sum(lvec, axis=-1, keepdims=True)

    acc = jnp.zeros((R, W_GROUP), F32)
    for g in range(nc, n_chunks):
        prefetch_after(g)
        wait(b, g)
        for i in range(BPS):
            acc = acc + _nt(p_scr[(g - nc) * BPS + i], block_of(g, i).astype(BF16))
    for j in range(ts):
        acc = acc + p_own[j] * vnew_ref[j:j + 1, :]
    yatt_ref[...] = _head_diag(acc / l, ts) * az_ref[...]

    mrows = _head_rows(mq_ref[...], ts).astype(BF16)
    s = _nn(mrows, memkt_ref[...].astype(BF16))
    p = jnp.exp(s - jnp.max(s, axis=-1, keepdims=True))
    lm = jnp.sum(p, axis=-1, keepdims=True)
    o = _nt(p.astype(BF16), memvt_ref[...].astype(BF16)) / lm
    ymem_ref[...] = _head_diag(o, ts) * mz_ref[...]


def _sample_moba(layer, page_table, q, knew, vnew, mq, az, mz, memkt, memvt, cache_kt, cache_vt):
    nb, n_pages = page_table.shape
    rows = q.shape[0]
    ts = rows // nb
    P = PAGES_PER_STEP
    assert n_pages % P == 0 and P % PAGES_PER_BLOCK == 0
    assert (2 * n_pages // P) % CHUNK_BUFS == 0
    past_len = n_pages * PAGE_SIZE
    R = N_HEADS * ts
    seq = pl.BlockSpec((ts, W_GROUP), lambda b, pt: (b, 0))
    mem = pl.BlockSpec((None, None, W_GROUP, N_MEM), lambda b, pt: (layer, b, 0, 0))
    hbm = pl.BlockSpec(memory_space=pl.ANY)
    out = jax.ShapeDtypeStruct((rows, W_GROUP), F32)
    grid_spec = pltpu.PrefetchScalarGridSpec(
        num_scalar_prefetch=1,
        grid=(nb,),
        in_specs=[seq, seq, seq, seq, seq, seq, mem, mem, hbm, hbm],
        out_specs=[seq, seq],
        scratch_shapes=[
            pltpu.VMEM((CHUNK_BUFS, P, W_GROUP, PAGE_SIZE), F32),
            pltpu.SemaphoreType.DMA((CHUNK_BUFS,)),
            pltpu.VMEM((past_len // MOBA_BLOCK, R, MOBA_BLOCK), F32),
            pltpu.VMEM((past_len // MOBA_BLOCK, R, MOBA_BLOCK), BF16),
        ],
    )
    return pl.pallas_call(
        functools.partial(_sample_moba_kernel, layer, past_len),
        grid_spec=grid_spec,
        out_shape=[out, out],
        compiler_params=pltpu.CompilerParams(dimension_semantics=("arbitrary",),
                                             vmem_limit_bytes=VMEM_LIMIT),
        name="sample_moba",
    )(page_table, q, knew, vnew, mq, az, mz, memkt, memvt, cache_kt, cache_vt)


def _same_head_matrix():
    r = lax.broadcasted_iota(jnp.int32, (W_GROUP, W_GROUP), 0) // HEAD_DIM
    c = lax.broadcasted_iota(jnp.int32, (W_GROUP, W_GROUP), 1) // HEAD_DIM
    return (r == c).astype(BF16)


def _pool_block_diag(pool_w_l):
    out = jnp.zeros((W_GROUP, W_GROUP), pool_w_l.dtype)
    for g in range(len(POOL_WINDOWS)):
        out = out.at[g * 64:(g + 1) * 64, g * 64:(g + 1) * 64].set(pool_w_l[g])
    return out


def _tokens_minor(a):
    lead = a.shape[:-3]
    n = len(lead)
    perm = tuple(range(n)) + (n + 1, n + 2, n)
    return a.transpose(perm).reshape(lead + (W_GROUP, a.shape[-3]))


def _tokens_major(a_t):
    lead = a_t.shape[:-2]
    n = len(lead)
    perm = tuple(range(n)) + (n + 2, n, n + 1)
    return a_t.reshape(lead + (N_HEADS, HEAD_DIM, a_t.shape[-1])).transpose(perm)


def kernel(x_prompt, x_sample, cache_k, cache_v, page_table, state_conv, state_pool, cache_mem_k, cache_mem_v,
           mem_prompt, g_norm, w_in, w_out, conv_w, conv_b, pool_w, pool_scale, g_q, g_k, g_mq, g_mk, g_mem,
           w_mem_kv):
    depth = w_in.shape[0]
    bp, seq, _ = x_prompt.shape
    bs, ts, _ = x_sample.shape
    n_pages = page_table.shape[1]
    past_len = n_pages * PAGE_SIZE
    bd = _same_head_matrix()
    tile4 = lambda g: jnp.tile(g, (1, N_HEADS))[:, None, :]

    w_in_bf = jnp.concatenate([w_in[:, :, g * W_GROUP:(g + 1) * W_GROUP].astype(BF16) for g in SPLIT_ORDER], axis=2)
    w_out_bf = w_out.astype(BF16)
    g_q_t, g_k_t, g_mq_t, g_mk_t = tile4(g_q), tile4(g_k), tile4(g_mq), tile4(g_mk)
    layer_w = []
    for l in range(depth):
        layer_w.append(dict(
            g_norm=g_norm[l][None, :], w_in=w_in_bf[l], conv_w=conv_w[l], conv_b=conv_b[l][None, :],
            pool_w=_pool_block_diag(pool_w[l]).astype(BF16), pool_scale=pool_scale[l][None, :],
            g_q=g_q_t[l], g_k=g_k_t[l], g_mq=g_mq_t[l], bd=bd))

    mkt_all, mvt_all, mk_bf, mvaug = _memkv(mem_prompt, g_mem[:, None, :], w_mem_kv.astype(BF16), g_mk_t, bd)

    cache_kt = _tokens_minor(cache_k)
    cache_vt = _tokens_minor(cache_v)
    memkt_s = _tokens_minor(cache_mem_k)
    memvt_s = _tokens_minor(cache_mem_v)

    xp = x_prompt
    xs = x_sample.reshape(bs * ts, D_MODEL)
    ks_l, vs_l, cp_l, cs_l, pp_l, ps_l = ([] for _ in range(6))
    kt, vt = None, None
    for l in range(depth):
        lw = layer_w[l]
        (q, kt, vt, kaug, vaug, kmean, yconv, ypool, ymem, az, cst, pst) = _prompt_front(
            xp, lw, mk_bf[l], mvaug[l], kt, vt)
        xp = _prompt_moba(q, kaug, vaug, kmean.reshape(bp, seq // ROW_TILE, W_GROUP), az,
                          xp, yconv, ypool, ymem, w_out_bf[l])
        cp_l.append(cst)
        pp_l.append(pst)
        (qs, ksn, vsn, mqs, azs, mzs, yconv_s, ypool_s, cst_s, pst_s) = _sample_front(
            xs, lw, state_conv[l], state_pool[l], past_len)
        yatt_s, ymem_s = _sample_moba(l, page_table, qs, ksn, vsn, mqs, azs, mzs, memkt_s, memvt_s,
                                      cache_kt, cache_vt)
        xs = _out_proj(xs, yconv_s, ypool_s, yatt_s, ymem_s, w_out_bf[l])
        ks_l.append(ksn.reshape(bs, ts, N_HEADS, HEAD_DIM))
        vs_l.append(vsn.reshape(bs, ts, N_HEADS, HEAD_DIM))
        cs_l.append(cst_s)
        ps_l.append(pst_s)

    return (xp, xs.reshape(bs, ts, D_MODEL), _tokens_major(kt), _tokens_major(vt), jnp.stack(ks_l), jnp.stack(vs_l),
            jnp.stack(cp_l), jnp.stack(cs_l), jnp.stack(pp_l), jnp.stack(ps_l),
            _tokens_major(mkt_all), _tokens_major(mvt_all))
```

```python
import functools
import struct

import jax
import jax.numpy as jnp
from jax import lax
from jax.experimental import pallas as pl
from jax.experimental.pallas import tpu as pltpu

F32 = jnp.float32
BF16 = jnp.bfloat16

D_MODEL = 1024
W_GROUP = 256
N_HEADS = 4
HEAD_DIM = 64
N_MEM = 256
CONV_W = 3
POOL_WINDOWS = (2, 4, 8, 16)
POOL_STATE = 15
MOBA_BLOCK = 256
MOBA_TOPK = 3
PAGE_SIZE = 128
N_IN_SPLITS = 12
EPS = 1e-6
NEG_INF = float("-inf")

ROW_TILE = 256
FRONT_TILES = 4
MOBA_TILES = 2
POOL_HALO = 32
CONV_HALO = 8
PAGES_PER_STEP = 16
CHUNK_BUFS = 4
PAGES_PER_BLOCK = MOBA_BLOCK // PAGE_SIZE
VMEM_LIMIT = 56 * 1024 * 1024

SPLIT_ORDER = (10, 7, 6, 8, 11, 9, 4, 5, 2, 0, 1, 3)
SPLIT_COLUMN = {g: c for c, g in enumerate(SPLIT_ORDER)}

MAX_BLOCKS = 32
AUG_LANES = 64
K_GROUP_LANES = HEAD_DIM + AUG_LANES
K_AUG_LANES = N_HEADS * K_GROUP_LANES
V_AUG_ROWS = 80
MASKED = -1e30


def _bf16_round(x):
    bits = struct.unpack("<I", struct.pack("<f", x))[0]
    bits = (bits + 0x7FFF + ((bits >> 16) & 1)) & 0xFFFF0000
    return struct.unpack("<f", struct.pack("<I", bits))[0]


LOG2E = 1.4426950408889634
LOG2E_HI = _bf16_round(LOG2E)
LOG2E_LO = LOG2E - LOG2E_HI


def _slope(h):
    return 2.0 ** (-8.0 * (h + 1) / N_HEADS)


def _nt(a, b):
    return lax.dot_general(a, b, (((1,), (1,)), ((), ())), preferred_element_type=F32)


def _split_bf16(x):
    hi = x.astype(BF16)
    lo = (x - hi.astype(F32)).astype(BF16)
    return hi, lo


def _dot3(a, b, dot):
    ah, al = _split_bf16(a)
    bh, bl = _split_bf16(b)
    return dot(ah, bh) + dot(ah, bl) + dot(al, bh)


def _nn(a, b):
    return jnp.dot(a, b, preferred_element_type=F32)


def _rms(x, g):
    ms = jnp.mean(x * x, axis=-1, keepdims=True)
    return x * lax.rsqrt(ms + EPS) * g


def _head_rstd(x, bd):
    hi, lo = _split_bf16(x * x)
    ssq = _nn(hi, bd) + _nn(lo, bd)
    return lax.rsqrt(ssq * (1.0 / HEAD_DIM) + EPS)


def _head_rms(x, g, bd):
    return x * _head_rstd(x, bd) * g


def _silu(z):
    return z / (1.0 + jnp.exp(-z))


def _top3_mask(gate, blk_iota, limit, axis):
    sel = jnp.zeros(gate.shape, F32)
    g = gate
    for _ in range(MOBA_TOPK):
        m = jnp.max(g, axis=axis, keepdims=True)
        idx = jnp.min(jnp.where(g == m, blk_iota, 1e9), axis=axis, keepdims=True)
        pick = blk_iota == idx
        sel = jnp.where(pick & (idx < limit), 1.0, sel)
        g = jnp.where(pick, NEG_INF, g)
    return sel


def _pool_lane_consts():
    lane = lax.broadcasted_iota(jnp.int32, (1, W_GROUP), 1)
    w = jnp.where(lane < 64, 2, jnp.where(lane < 128, 4, jnp.where(lane < 192, 8, 16)))
    return lane, w


def _value_groups(vt, n_tok):
    ones_row = (lax.broadcasted_iota(jnp.int32, (V_AUG_ROWS - HEAD_DIM, n_tok), 0) == 0).astype(F32)
    parts = []
    for hd in range(N_HEADS):
        parts += [vt[hd * HEAD_DIM:(hd + 1) * HEAD_DIM, :], ones_row]
    return jnp.concatenate(parts, axis=0).astype(BF16)


def _memkv_kernel(mem_ref, g_ref, w_ref, gmk_ref, bd_ref, mkt_ref, mvt_ref, mkb_ref, mvaug_ref):
    h = _rms(mem_ref[...], g_ref[...]).astype(BF16)
    kv = _nn(h, w_ref[...])
    mk = _head_rms(kv[:, :W_GROUP], gmk_ref[...], bd_ref[...])
    mvt = kv[:, W_GROUP:].T
    mkt_ref[...] = mk.T
    mvt_ref[...] = mvt
    mkb_ref[...] = mk.astype(BF16)
    mvaug_ref[...] = _value_groups(mvt, N_MEM)


def _memkv(mem_prompt, g_mem, w_mem_kv_bf, g_mk_t, bd):
    depth = g_mem.shape[0]
    batch = mem_prompt.shape[0]
    out = lambda dt: jax.ShapeDtypeStruct((depth, batch, N_MEM, W_GROUP), dt)
    ospec = pl.BlockSpec((None, None, N_MEM, W_GROUP), lambda l, b: (l, b, 0, 0))
    vaug_rows = N_HEADS * V_AUG_ROWS
    return pl.pallas_call(
        _memkv_kernel,
        grid=(depth, batch),
        in_specs=[
            pl.BlockSpec((None, N_MEM, D_MODEL), lambda l, b: (b, 0, 0)),
            pl.BlockSpec((None, 1, D_MODEL), lambda l, b: (l, 0, 0)),
            pl.BlockSpec((None, D_MODEL, 2 * W_GROUP), lambda l, b: (l, 0, 0)),
            pl.BlockSpec((None, 1, W_GROUP), lambda l, b: (l, 0, 0)),
            pl.BlockSpec((W_GROUP, W_GROUP), lambda l, b: (0, 0)),
        ],
        out_specs=[ospec, ospec, ospec,
                   pl.BlockSpec((None, None, vaug_rows, N_MEM), lambda l, b: (l, b, 0, 0))],
        out_shape=[out(F32), out(F32), out(BF16),
                   jax.ShapeDtypeStruct((depth, batch, vaug_rows, N_MEM), BF16)],
        compiler_params=pltpu.CompilerParams(dimension_semantics=("arbitrary", "arbitrary")),
        name="memkv",
    )(mem_prompt, g_mem, w_mem_kv_bf, g_mk_t, bd)


def _prompt_front_kernel(n_prev, x_ref, gn_ref, win_ref, cw_ref, cb_ref, pw_ref, ps_ref, gq_ref, gk_ref, gmq_ref,
                         bd_ref, memk_ref, memvaug_ref, *refs):
    if n_prev:
        ktprev_ref, vtprev_ref, *refs = refs
    (q_ref, kt_ref, vt_ref, kaug_ref, vaug_ref, kmean_ref, yconv_ref, ypool_ref, ymem_ref, az_ref,
     cstate_ref, pstate_ref, u_scr, e_scr, s2_scr, s4_scr, s8_scr, proj_scr, ot_scr, ms_scr, rstd_scr) = refs
    if n_prev:
        kt_ref[0:n_prev] = ktprev_ref[...]
        vt_ref[0:n_prev] = vtprev_ref[...]
    step = pl.program_id(1)
    T = ROW_TILE
    H = POOL_HALO
    NT = FRONT_TILES

    @pl.when(step == 0)
    def _():
        u_scr[0:CONV_HALO, :] = jnp.zeros((CONV_HALO, W_GROUP), F32)
        e_scr[0:H, :] = jnp.zeros((H, W_GROUP), F32)

    bd = bd_ref[...]
    cw = cw_ref[...]
    lane, wlane = _pool_lane_consts()
    lane64 = lax.broadcasted_iota(jnp.int32, (T, AUG_LANES), 1)
    key_idx = lax.broadcasted_iota(jnp.int32, (T, AUG_LANES), 0).astype(F32)
    slope_lanes = (lane64 == MAX_BLOCKS) | (lane64 == MAX_BLOCKS + 1)

    for sub in range(NT):
        t = step * NT + sub
        rows = slice(sub * T, (sub + 1) * T)

        h = _rms(x_ref[rows, :], gn_ref[...]).astype(BF16)
        proj_scr[sub] = _nn(h, win_ref[...])

        def proj(g, sub=sub):
            c = SPLIT_COLUMN[g]
            return proj_scr[sub, :, c * W_GROUP:(c + 1) * W_GROUP]

        for j, g in enumerate((6, 7, 10)):
            rstd_scr[sub, j] = _head_rstd(proj(g), bd)

        ub = CONV_HALO + sub * T
        u = proj(2) * proj(0)
        u_scr[ub:ub + T, :] = u
        conv = (cb_ref[...] + cw[0:1] * u_scr[ub - 2:ub - 2 + T, :]
                + cw[1:2] * u_scr[ub - 1:ub - 1 + T, :] + cw[2:3] * u)
        yconv_ref[rows, :] = (proj(1) * conv * _silu(proj(3))).astype(BF16)

        eb = sub * T
        pv = proj(4)
        e_scr[eb + H:eb + H + T, :] = pv
        s2_scr[sub, 8:H + T, :] = e_scr[eb + 8:eb + H + T, :] + e_scr[eb + 7:eb + H + T - 1, :]
        s4_scr[sub, 16:H + T, :] = s2_scr[sub, 16:H + T, :] + s2_scr[sub, 14:H + T - 2, :]
        s8_scr[sub, 24:H + T, :] = s4_scr[sub, 24:H + T, :] + s4_scr[sub, 20:H + T - 4, :]
        s16 = s8_scr[sub, H:H + T, :] + s8_scr[sub, H - 8:H + T - 8, :]
        wsum = jnp.where(lane < 64, s2_scr[sub, H:H + T, :],
                         jnp.where(lane < 128, s4_scr[sub, H:H + T, :],
                                   jnp.where(lane < 192, s8_scr[sub, H:H + T, :], s16)))
        pos = t * T + lax.broadcasted_iota(jnp.int32, (T, 1), 0)
        cnt = jnp.minimum(pos + 1, wlane).astype(F32)
        pooled = (wsum / cnt - pv).astype(BF16)
        pool_out = _nn(pooled, pw_ref[...]) * ps_ref[...]
        ypool_ref[rows, :] = (pool_out * _silu(proj(5))).astype(BF16)

        q_ref[rows, :] = proj(6) * rstd_scr[sub, 0] * gq_ref[...] * (HEAD_DIM ** -0.5)
        k = proj(7) * rstd_scr[sub, 1] * gk_ref[...]
        v = proj(8)
        vt = v.T
        kt_ref[n_prev, :, rows] = k.T
        vt_ref[n_prev, :, rows] = vt
        kmean_ref[sub] = jnp.mean(k, axis=0, keepdims=True)
        onehot = (lane64 == t).astype(F32)
        kparts = []
        for hd in range(N_HEADS):
            kparts += [k[:, hd * HEAD_DIM:(hd + 1) * HEAD_DIM],
                       jnp.where(slope_lanes, _slope(hd) * key_idx, onehot)]
        kaug_ref[sub] = jnp.concatenate(kparts, axis=1).astype(BF16)
        vaug_ref[sub] = _value_groups(vt, T)
        az_ref[rows, :] = _silu(proj(9)).astype(BF16)

        mqt = (proj(10) * rstd_scr[sub, 2] * gmq_ref[...] * (HEAD_DIM ** -0.5 * LOG2E)).T.astype(BF16)
        mmax = []
        for hd in range(N_HEADS):
            s = _nn(memk_ref[:, hd * HEAD_DIM:(hd + 1) * HEAD_DIM], mqt[hd * HEAD_DIM:(hd + 1) * HEAD_DIM, :])
            ms_scr[sub, hd] = s
            mmax.append(jnp.max(s, axis=0, keepdims=True))
        for hd in range(N_HEADS):
            hs = slice(hd * HEAD_DIM, (hd + 1) * HEAD_DIM)
            p = jnp.exp2(ms_scr[sub, hd] - mmax[hd])
            acc = _nn(memvaug_ref[hd * V_AUG_ROWS:(hd + 1) * V_AUG_ROWS, :], p.astype(BF16))
            ot_scr[sub, hs, :] = acc[0:HEAD_DIM] / acc[HEAD_DIM:HEAD_DIM + 1]
        ymem_ref[rows, :] = (ot_scr[sub].T * _silu(proj(11))).astype(BF16)

    last = NT * T
    cstate_ref[...] = u_scr[CONV_HALO + last - 2:CONV_HALO + last, :]
    pstate_ref[...] = e_scr[H + last - POOL_STATE:H + last, :]
    u_scr[0:CONV_HALO, :] = u_scr[last:last + CONV_HALO, :]
    e_scr[0:H, :] = e_scr[last:last + H, :]


def _prompt_front(x, lw, memk_bf, memvaug, kt_prev, vt_prev):
    batch, seq, _ = x.shape
    nt = seq // ROW_TILE
    NT = FRONT_TILES
    R = NT * ROW_TILE
    assert nt <= MAX_BLOCKS and nt % NT == 0
    n_prev = 0 if kt_prev is None else kt_prev.shape[0]
    row = lambda b, t: (b, t, 0)
    const2 = lambda b, t: (0, 0)
    tile = lambda dt: jax.ShapeDtypeStruct((batch, seq, W_GROUP), dt)
    tile_t = jax.ShapeDtypeStruct((n_prev + 1, batch, W_GROUP, seq), F32)
    tile_spec = pl.BlockSpec((None, R, W_GROUP), row)
    planes_spec = lambda n: pl.BlockSpec((n, None, W_GROUP, R), lambda b, t: (0, b, 0, t))
    tile_t_spec = planes_spec(n_prev + 1)
    kaug_spec = pl.BlockSpec((None, NT, ROW_TILE, K_AUG_LANES), lambda b, t: (b, t, 0, 0))
    vaug_spec = pl.BlockSpec((None, NT, N_HEADS * V_AUG_ROWS, ROW_TILE), lambda b, t: (b, t, 0, 0))
    vec = pl.BlockSpec((1, W_GROUP), const2)
    prev_specs = [planes_spec(n_prev)] * 2 if n_prev else []
    prev_args = (kt_prev, vt_prev) if n_prev else ()
    return pl.pallas_call(
        functools.partial(_prompt_front_kernel, n_prev),
        grid=(batch, nt // NT),
        in_specs=[
            pl.BlockSpec((None, R, D_MODEL), row),
            pl.BlockSpec((1, D_MODEL), const2),
            pl.BlockSpec((D_MODEL, N_IN_SPLITS * W_GROUP), const2),
            pl.BlockSpec((CONV_W, W_GROUP), const2),
            vec,
            pl.BlockSpec((W_GROUP, W_GROUP), const2),
            vec, vec, vec, vec,
            pl.BlockSpec((W_GROUP, W_GROUP), const2),
            pl.BlockSpec((None, N_MEM, W_GROUP), lambda b, t: (b, 0, 0)),
            pl.BlockSpec((None, N_HEADS * V_AUG_ROWS, N_MEM), lambda b, t: (b, 0, 0)),
        ] + prev_specs,
        out_specs=[
            tile_spec, tile_t_spec, tile_t_spec, kaug_spec, vaug_spec,
            pl.BlockSpec((None, NT, 1, W_GROUP), lambda b, t: (b, t, 0, 0)),
            tile_spec, tile_spec, tile_spec, tile_spec,
            pl.BlockSpec((None, CONV_W - 1, W_GROUP), lambda b, t: (b, 0, 0)),
            pl.BlockSpec((None, POOL_STATE, W_GROUP), lambda b, t: (b, 0, 0)),
        ],
        out_shape=[
            tile(F32), tile_t, tile_t,
            jax.ShapeDtypeStruct((batch, nt, ROW_TILE, K_AUG_LANES), BF16),
            jax.ShapeDtypeStruct((batch, nt, N_HEADS * V_AUG_ROWS, ROW_TILE), BF16),
            jax.ShapeDtypeStruct((batch, nt, 1, W_GROUP), F32),
            tile(BF16), tile(BF16), tile(BF16), tile(BF16),
            jax.ShapeDtypeStruct((batch, CONV_W - 1, W_GROUP), F32),
            jax.ShapeDtypeStruct((batch, POOL_STATE, W_GROUP), F32),
        ],
        scratch_shapes=[
            pltpu.VMEM((CONV_HALO + R, W_GROUP), F32),
            pltpu.VMEM((POOL_HALO + R, W_GROUP), F32),
            pltpu.VMEM((NT, POOL_HALO + ROW_TILE, W_GROUP), F32),
            pltpu.VMEM((NT, POOL_HALO + ROW_TILE, W_GROUP), F32),
            pltpu.VMEM((NT, POOL_HALO + ROW_TILE, W_GROUP), F32),
            pltpu.VMEM((NT, ROW_TILE, N_IN_SPLITS * W_GROUP), F32),
            pltpu.VMEM((NT, W_GROUP, ROW_TILE), F32),
            pltpu.VMEM((NT, N_HEADS, N_MEM, ROW_TILE), F32),
            pltpu.VMEM((NT, 3, ROW_TILE, W_GROUP), F32),
        ],
        compiler_params=pltpu.CompilerParams(dimension_semantics=("arbitrary", "arbitrary"),
                                             vmem_limit_bytes=VMEM_LIMIT),
        name="prompt_front",
    )(x, lw["g_norm"], lw["w_in"], lw["conv_w"], lw["conv_b"], lw["pool_w"], lw["pool_scale"],
      lw["g_q"], lw["g_k"], lw["g_mq"], lw["bd"], memk_bf, memvaug, *prev_args)


def _prompt_moba_kernel(q_ref, kaug_ref, vaug_ref, kmean_ref, az_ref, x_ref, yconv_ref, ypool_ref, ymem_ref, wout_ref,
                        y_ref, qaug_scr, acc_scr, outt_scr, s_scr):
    step = pl.program_id(1)
    T = ROW_TILE
    MT = MOBA_TILES
    nb = kmean_ref.shape[0]
    km = kmean_ref[...]
    blk_iota = lax.broadcasted_iota(jnp.int32, (nb, T), 0)
    blk_f = blk_iota.astype(F32)
    row8 = lax.broadcasted_iota(jnp.int32, (8, T), 0)
    log2e_rows = jnp.where(row8 == 0, LOG2E_HI, jnp.where(row8 == 1, LOG2E_LO, 0.0))
    pad_rows = jnp.zeros((K_GROUP_LANES - HEAD_DIM - MAX_BLOCKS - 8, T), F32)
    causal = lax.broadcasted_iota(jnp.int32, (T, T), 0) <= lax.broadcasted_iota(jnp.int32, (T, T), 1)
    chains = [(tile, hd) for tile in range(MT) for hd in range(N_HEADS)]

    def tile_block(tile):
        return step * MT + tile

    def kgroup(n, hd):
        return kaug_ref[n, :, hd * K_GROUP_LANES:(hd + 1) * K_GROUP_LANES]

    def vgroup(n, hd):
        return vaug_ref[n, hd * V_AUG_ROWS:(hd + 1) * V_AUG_ROWS, :]

    def block_term(n, i, hd):
        return (n - i).astype(F32) * (_slope(hd) * MOBA_BLOCK * LOG2E)

    def online_update(s, mx, cn, m, acc, vg):
        m_new = jnp.maximum(m, mx + cn)
        p = jnp.exp2(s - (m_new - cn))
        return m_new, jnp.exp2(m - m_new) * acc + _nn(vg, p.astype(BF16))

    for tile in range(MT):
        i = tile_block(tile)
        i_f = i.astype(F32)
        qt = q_ref[tile * T:(tile + 1) * T, :].T
        for hd in range(N_HEADS):
            hs = slice(hd * HEAD_DIM, (hd + 1) * HEAD_DIM)
            qt_h = qt[hs, :]
            gate = jnp.where(blk_f < i_f, _dot3(km[:, hs], qt_h, _nn), NEG_INF)
            sel = _top3_mask(gate, blk_f, i_f, axis=0)
            bias = jnp.where((sel > 0.0) | (blk_iota == i), 0.0, MASKED)
            if nb < MAX_BLOCKS:
                bias = jnp.concatenate([bias, jnp.zeros((MAX_BLOCKS - nb, T), F32)], axis=0)
            qaug_scr[tile * N_HEADS + hd] = jnp.concatenate(
                [qt_h * LOG2E, bias, log2e_rows, pad_rows], axis=0).astype(BF16)
    m0, mx0 = [], []
    for c, (tile, hd) in enumerate(chains):
        s = jnp.where(causal, _nn(kgroup(tile_block(tile), hd), qaug_scr[c]), NEG_INF)
        s_scr[1, c] = s
        m0.append(jnp.max(s, axis=0, keepdims=True))
    for c, (tile, hd) in enumerate(chains):
        s = _nn(kgroup(0, hd), qaug_scr[c])
        s_scr[0, c] = s
        mx0.append(jnp.max(s, axis=0, keepdims=True))
    for c, (tile, hd) in enumerate(chains):
        acc_scr[c] = _nn(vgroup(tile_block(tile), hd), jnp.exp2(s_scr[1, c] - m0[c]).astype(BF16))

    def stage(n, cur, nxt, ms, mxs):
        ms_out, mxs_out = [], []
        for c, (tile, hd) in enumerate(chains):
            s_next = _nn(kgroup(n + 1, hd), qaug_scr[c])
            s_scr[nxt, c] = s_next
            mxs_out.append(jnp.max(s_next, axis=0, keepdims=True))
            m_new, acc_scr[c] = online_update(s_scr[cur, c], mxs[c], block_term(n, tile_block(tile), hd), ms[c],
                                              acc_scr[c], vgroup(n, hd))
            ms_out.append(m_new)
        return tuple(ms_out), tuple(mxs_out)

    def body(j, carry):
        ms, mxs = stage(2 * j, 0, 1, *carry)
        return stage(2 * j + 1, 1, 0, ms, mxs)

    first = tile_block(0)
    ms, mxs = lax.fori_loop(0, first // 2, body, (tuple(m0), tuple(mx0)))
    for c, (tile, hd) in enumerate(chains):
        if tile == 1:
            _, acc_scr[c] = online_update(s_scr[0, c], mxs[c], block_term(first, tile_block(tile), hd), ms[c],
                                          acc_scr[c], vgroup(first, hd))
    for tile in range(MT):
        rows = slice(tile * T, (tile + 1) * T)
        for hd in range(N_HEADS):
            acc = acc_scr[tile * N_HEADS + hd]
            outt_scr[tile, hd * HEAD_DIM:(hd + 1) * HEAD_DIM, :] = acc[0:HEAD_DIM] / acc[HEAD_DIM:HEAD_DIM + 1]
        yatt = (outt_scr[tile].T * az_ref[rows, :].astype(F32)).astype(BF16)
        y = x_ref[rows, :]
        for g, part in enumerate((yconv_ref[rows, :], ypool_ref[rows, :], yatt, ymem_ref[rows, :])):
            y = y + _nn(part, wout_ref[g * W_GROUP:(g + 1) * W_GROUP, :])
        y_ref[rows, :] = y


def _prompt_moba(q, kaug, vaug, kmean, az, x, yconv, ypool, ymem, w_out_bf):
    batch, seq, _ = q.shape
    nt = seq // ROW_TILE
    MT = MOBA_TILES
    assert nt % MT == 0 and MT == 2
    R = MT * ROW_TILE
    row = lambda b, t: (b, t, 0)
    part = pl.BlockSpec((None, R, W_GROUP), row)
    return pl.pallas_call(
        _prompt_moba_kernel,
        grid=(batch, nt // MT),
        in_specs=[
            part,
            pl.BlockSpec((None, nt, ROW_TILE, K_AUG_LANES), lambda b, t: (b, 0, 0, 0)),
            pl.BlockSpec((None, nt, N_HEADS * V_AUG_ROWS, ROW_TILE), lambda b, t: (b, 0, 0, 0)),
            pl.BlockSpec((None, nt, W_GROUP), lambda b, t: (b, 0, 0)),
            part,
            pl.BlockSpec((None, R, D_MODEL), row),
            part, part, part,
            pl.BlockSpec((D_MODEL, D_MODEL), lambda b, t: (0, 0)),
        ],
        out_specs=pl.BlockSpec((None, R, D_MODEL), row),
        out_shape=jax.ShapeDtypeStruct((batch, seq, D_MODEL), F32),
        scratch_shapes=[
            pltpu.VMEM((MT * N_HEADS, K_GROUP_LANES, ROW_TILE), BF16),
            pltpu.VMEM((MT * N_HEADS, V_AUG_ROWS, ROW_TILE), F32),
            pltpu.VMEM((MT, W_GROUP, ROW_TILE), F32),
            pltpu.VMEM((2, MT * N_HEADS, MOBA_BLOCK, ROW_TILE), F32),
        ],
        compiler_params=pltpu.CompilerParams(dimension_semantics=("arbitrary", "arbitrary"),
                                             vmem_limit_bytes=VMEM_LIMIT),
        name="prompt_moba",
    )(q, kaug, vaug, kmean, az, x, yconv, ypool, ymem, w_out_bf)


def _out_kernel(x_ref, yconv_ref, ypool_ref, yatt_ref, ymem_ref, w_ref, y_ref):
    acc = x_ref[...]
    for g, ref in enumerate((yconv_ref, ypool_ref, yatt_ref, ymem_ref)):
        acc = acc + _nn(ref[...].astype(BF16), w_ref[g * W_GROUP:(g + 1) * W_GROUP, :])
    y_ref[...] = acc


def _out_proj(x2, yconv, ypool, yatt, ymem, w_out_bf):
    rows = x2.shape[0]
    row = lambda r: (r, 0)
    part = pl.BlockSpec((ROW_TILE, W_GROUP), row)
    return pl.pallas_call(
        _out_kernel,
        grid=(rows // ROW_TILE,),
        in_specs=[pl.BlockSpec((ROW_TILE, D_MODEL), row), part, part, part, part,
                  pl.BlockSpec((D_MODEL, D_MODEL), lambda r: (0, 0))],
        out_specs=pl.BlockSpec((ROW_TILE, D_MODEL), row),
        out_shape=jax.ShapeDtypeStruct((rows, D_MODEL), F32),
        compiler_params=pltpu.CompilerParams(dimension_semantics=("arbitrary",),
                                             vmem_limit_bytes=VMEM_LIMIT),
        name="out_proj",
    )(x2, yconv, ypool, yatt, ymem, w_out_bf)


def _sample_front_kernel(start, x_ref, gn_ref, win_ref, cw_ref, cb_ref, pw_ref, ps_ref, gq_ref, gk_ref, gmq_ref,
                         bd_ref, sconv_ref, spool_ref,
                         q_ref, k_ref, v_ref, mq_ref, az_ref, mz_ref, yconv_ref, ypool_ref, cstate_ref, pstate_ref,
                         u_scr, e_scr):
    nb, ts = sconv_ref.shape[0], u_scr.shape[1] - CONV_HALO
    rows = nb * ts
    h = _rms(x_ref[...], gn_ref[...]).astype(BF16)
    bd = bd_ref[...]

    def proj(g):
        c = SPLIT_COLUMN[g]
        return _nn(h, win_ref[:, c * W_GROUP:(c + 1) * W_GROUP])

    def to3(a):
        return a.reshape(nb, ts, W_GROUP)

    def to2(a):
        return a.reshape(rows, W_GROUP)

    u = proj(2) * proj(0)
    u_scr[:, 0:CONV_HALO - 2, :] = jnp.zeros((nb, CONV_HALO - 2, W_GROUP), F32)
    u_scr[:, CONV_HALO - 2:CONV_HALO, :] = sconv_ref[...]
    u_scr[:, CONV_HALO:CONV_HALO + ts, :] = to3(u)
    cw = cw_ref[...]
    conv3 = (cw[0:1] * u_scr[:, CONV_HALO - 2:CONV_HALO - 2 + ts, :]
             + cw[1:2] * u_scr[:, CONV_HALO - 1:CONV_HALO - 1 + ts, :]
             + cw[2:3] * u_scr[:, CONV_HALO:CONV_HALO + ts, :])
    conv = cb_ref[...] + to2(conv3)
    yconv_ref[...] = proj(1) * conv * _silu(proj(3))
    cstate_ref[...] = u_scr[:, CONV_HALO + ts - 2:CONV_HALO + ts, :]

    H = 16
    pv = proj(4)
    e_scr[:, 0:1, :] = jnp.zeros((nb, 1, W_GROUP), F32)
    e_scr[:, 1:H, :] = spool_ref[...]
    e_scr[:, H:H + ts, :] = to3(pv)
    lane, wlane = _pool_lane_consts()
    wsum = jnp.zeros((nb, ts, W_GROUP), F32)
    for j in range(max(POOL_WINDOWS)):
        shifted = e_scr[:, H - j:H - j + ts, :]
        wsum = wsum + (shifted if j < min(POOL_WINDOWS) else jnp.where(wlane > j, shifted, 0.0))
    pos = start + lax.broadcasted_iota(jnp.int32, (ts, 1), 0)
    cnt = jnp.minimum(pos + 1, wlane).astype(F32)
    pooled = (to2(wsum / cnt) - pv).astype(BF16)
    pool_out = _nn(pooled, pw_ref[...]) * ps_ref[...]
    ypool_ref[...] = pool_out * _silu(proj(5))
    pstate_ref[...] = e_scr[:, H + ts - POOL_STATE:H + ts, :]

    q_ref[...] = _head_rms(proj(6), gq_ref[...], bd) * (HEAD_DIM ** -0.5)
    k_ref[...] = _head_rms(proj(7), gk_ref[...], bd)
    v_ref[...] = proj(8)
    az_ref[...] = _silu(proj(9))
    mq_ref[...] = _head_rms(proj(10), gmq_ref[...], bd) * (HEAD_DIM ** -0.5)
    mz_ref[...] = _silu(proj(11))


def _sample_front(x2, lw, sconv, spool, start):
    nb = sconv.shape[0]
    rows = x2.shape[0]
    ts = rows // nb
    full = lambda shape: pl.BlockSpec(shape, lambda i, n=len(shape): (0,) * n)
    t2 = jax.ShapeDtypeStruct((rows, W_GROUP), F32)
    return pl.pallas_call(
        functools.partial(_sample_front_kernel, start),
        grid=(1,),
        in_specs=[
            full((rows, D_MODEL)), full((1, D_MODEL)), full((D_MODEL, N_IN_SPLITS * W_GROUP)),
            full((CONV_W, W_GROUP)), full((1, W_GROUP)), full((W_GROUP, W_GROUP)),
            full((1, W_GROUP)), full((1, W_GROUP)), full((1, W_GROUP)), full((1, W_GROUP)),
            full((W_GROUP, W_GROUP)),
            full((nb, CONV_W - 1, W_GROUP)), full((nb, POOL_STATE, W_GROUP)),
        ],
        out_specs=[full((rows, W_GROUP))] * 8 + [full((nb, CONV_W - 1, W_GROUP)), full((nb, POOL_STATE, W_GROUP))],
        out_shape=[t2] * 8 + [jax.ShapeDtypeStruct((nb, CONV_W - 1, W_GROUP), F32),
                              jax.ShapeDtypeStruct((nb, POOL_STATE, W_GROUP), F32)],
        scratch_shapes=[
            pltpu.VMEM((nb, CONV_HALO + ts, W_GROUP), F32),
            pltpu.VMEM((nb, 16 + ts, W_GROUP), F32),
        ],
        compiler_params=pltpu.CompilerParams(dimension_semantics=("arbitrary",),
                                             vmem_limit_bytes=VMEM_LIMIT),
        name="sample_front",
    )(x2, lw["g_norm"], lw["w_in"], lw["conv_w"], lw["conv_b"], lw["pool_w"], lw["pool_scale"],
      lw["g_q"], lw["g_k"], lw["g_mq"], lw["bd"], sconv, spool)


def _head_rows(a, ts):
    r = lax.broadcasted_iota(jnp.int32, (N_HEADS * ts, W_GROUP), 0) // ts
    c = lax.broadcasted_iota(jnp.int32, (N_HEADS * ts, W_GROUP), 1) // HEAD_DIM
    return jnp.where(r == c, jnp.concatenate([a] * N_HEADS, axis=0), 0.0)


def _head_diag(o, ts):
    r = lax.broadcasted_iota(jnp.int32, (N_HEADS * ts, W_GROUP), 0) // ts
    c = lax.broadcasted_iota(jnp.int32, (N_HEADS * ts, W_GROUP), 1) // HEAD_DIM
    o = jnp.where(r == c, o, 0.0)
    out = o[0:ts]
    for hd in range(1, N_HEADS):
        out = out + o[hd * ts:(hd + 1) * ts]
    return out


def _sample_moba_kernel(layer, past_len, pt_ref, q_ref, knew_ref, vnew_ref, mq_ref, az_ref, mz_ref, memkt_ref,
                        memvt_ref, ck_hbm, cv_hbm, yatt_ref, ymem_ref, buf, sem, s_scr, p_scr):
    P = PAGES_PER_STEP
    BPS = P // PAGES_PER_BLOCK
    b = pl.program_id(0)
    n_seqs = pl.num_programs(0)
    ts = q_ref.shape[0]
    R = N_HEADS * ts
    n_pages = past_len // PAGE_SIZE
    nblk = past_len // MOBA_BLOCK
    nc = n_pages // P
    n_chunks = 2 * nc

    def chunk_copies(seq, g):
        src = ck_hbm if g < nc else cv_hbm
        first = (g % nc) * P
        slot = g % CHUNK_BUFS
        return [pltpu.make_async_copy(src.at[layer, pt_ref[seq, first + j]], buf.at[slot, j], sem.at[slot])
                for j in range(P)]

    def start(seq, g):
        for j, cp in enumerate(chunk_copies(seq, g)):
            cp.start(priority=j % 2)

    def wait(seq, g):
        for cp in chunk_copies(seq, g):
            cp.wait()

    def prefetch_after(g):
        ahead = g + CHUNK_BUFS - 1
        if ahead < n_chunks:
            start(b, ahead)
        else:
            @pl.when(b + 1 < n_seqs)
            def _():
                start(b + 1, ahead - n_chunks)

    @pl.when(b == 0)
    def _():
        for g in range(CHUNK_BUFS - 1):
            start(b, g)

    qrows = _head_rows(q_ref[...], ts)
    row_t = (lax.broadcasted_iota(jnp.int32, (R, 1), 0) % ts).astype(F32)
    row_h = lax.broadcasted_iota(jnp.int32, (R, 1), 0) // ts
    slope = jnp.where(row_h == 0, _slope(0), jnp.where(row_h == 1, _slope(1),
                                                        jnp.where(row_h == 2, _slope(2), _slope(3))))

    qb = qrows.astype(BF16)
    lane8 = lax.broadcasted_iota(jnp.int32, (W_GROUP, BPS), 1)
    ksums = []
    def block_of(g, i):
        return jnp.concatenate([buf[g % CHUNK_BUFS, i * PAGES_PER_BLOCK + k] for k in range(PAGES_PER_BLOCK)], axis=1)

    for g in range(nc):
        prefetch_after(g)
        wait(b, g)
        sums = jnp.zeros((W_GROUP, BPS), F32)
        for i in range(BPS):
            kt_blk = block_of(g, i)
            s_scr[g * BPS + i] = _nn(qb, kt_blk.astype(BF16))
            tot = jnp.sum(kt_blk, axis=1, keepdims=True)
            sums = jnp.where(lane8 == i, tot, sums)
        ksums.append(sums)

    gate = jnp.concatenate([_dot3(qrows, ks * (1.0 / MOBA_BLOCK), _nn) for ks in ksums], axis=1)
    blk_iota = lax.broadcasted_iota(jnp.int32, (R, nblk), 1).astype(F32)
    sel = _top3_mask(gate, blk_iota, float(nblk), axis=-1)
    blk_term = jnp.where(sel > 0.0, slope * (blk_iota * MOBA_BLOCK - past_len), NEG_INF)
    key_term = slope * lax.broadcasted_iota(jnp.int32, (R, MOBA_BLOCK), 1).astype(F32)
    own = []
    m = jnp.full((R, 1), NEG_INF, F32)
    for j in range(ts):
        sj = jnp.sum(qrows * knew_ref[j:j + 1, :], axis=-1, keepdims=True) + slope * float(j)
        sj = jnp.where(row_t >= float(j), sj, NEG_INF)
        own.append(sj)
        m = jnp.maximum(m, sj)
    mvec = jnp.full((R, MOBA_BLOCK), NEG_INF, F32)
    for n in range(nblk):
        s = s_scr[n] + key_term + blk_term[:, n:n + 1]
        s_scr[n] = s
        mvec = jnp.maximum(mvec, s)
    m = jnp.maximum(m, jnp.max(mvec, axis=-1, keepdims=True))
    p_own = [jnp.exp(sj - m) for sj in own]
    lvec = jnp.zeros((R, MOBA_BLOCK), F32)
    for n in range(nblk):
        p = jnp.exp(s_scr[n] - m)
        p_scr[n] = p.astype(BF16)
        lvec = lvec + p
    l = sum(p_own) + jnp.sum(lvec, axis=-1, keepdims=True)

    acc = jnp.zeros((R, W_GROUP), F32)
    for g in range(nc, n_chunks):
        prefetch_after(g)
        wait(b, g)
        for i in range(BPS):
            acc = acc + _nt(p_scr[(g - nc) * BPS + i], block_of(g, i).astype(BF16))
    for j in range(ts):
        acc = acc + p_own[j] * vnew_ref[j:j + 1, :]
    yatt_ref[...] = _head_diag(acc / l, ts) * az_ref[...]

    mrows = _head_rows(mq_ref[...], ts).astype(BF16)
    s = _nn(mrows, memkt_ref[...].astype(BF16))
    p = jnp.exp(s - jnp.max(s, axis=-1, keepdims=True))
    lm = jnp.sum(p, axis=-1, keepdims=True)
    o = _nt(p.astype(BF16), memvt_ref[...].astype(BF16)) / lm
    ymem_ref[...] = _head_diag(o, ts) * mz_ref[...]


def _sample_moba(layer, page_table, q, knew, vnew, mq, az, mz, memkt, memvt, cache_kt, cache_vt):
    nb, n_pages = page_table.shape
    rows = q.shape[0]
    ts = rows // nb
    P = PAGES_PER_STEP
    assert n_pages % P == 0 and P % PAGES_PER_BLOCK == 0
    assert (2 * n_pages // P) % CHUNK_BUFS == 0
    past_len = n_pages * PAGE_SIZE
    R = N_HEADS * ts
    seq = pl.BlockSpec((ts, W_GROUP), lambda b, pt: (b, 0))
    mem = pl.BlockSpec((None, None, W_GROUP, N_MEM), lambda b, pt: (layer, b, 0, 0))
    hbm = pl.BlockSpec(memory_space=pl.ANY)
    out = jax.ShapeDtypeStruct((rows, W_GROUP), F32)
    grid_spec = pltpu.PrefetchScalarGridSpec(
        num_scalar_prefetch=1,
        grid=(nb,),
        in_specs=[seq, seq, seq, seq, seq, seq, mem, mem, hbm, hbm],
        out_specs=[seq, seq],
        scratch_shapes=[
            pltpu.VMEM((CHUNK_BUFS, P, W_GROUP, PAGE_SIZE), F32),
            pltpu.SemaphoreType.DMA((CHUNK_BUFS,)),
            pltpu.VMEM((past_len // MOBA_BLOCK, R, MOBA_BLOCK), F32),
            pltpu.VMEM((past_len // MOBA_BLOCK, R, MOBA_BLOCK), BF16),
        ],
    )
    return pl.pallas_call(
        functools.partial(_sample_moba_kernel, layer, past_len),
        grid_spec=grid_spec,
        out_shape=[out, out],
        compiler_params=pltpu.CompilerParams(dimension_semantics=("arbitrary",),
                                             vmem_limit_bytes=VMEM_LIMIT),
        name="sample_moba",
    )(page_table, q, knew, vnew, mq, az, mz, memkt, memvt, cache_kt, cache_vt)


def _same_head_matrix():
    r = lax.broadcasted_iota(jnp.int32, (W_GROUP, W_GROUP), 0) // HEAD_DIM
    c = lax.broadcasted_iota(jnp.int32, (W_GROUP, W_GROUP), 1) // HEAD_DIM
    return (r == c).astype(BF16)


def _pool_block_diag(pool_w_l):
    out = jnp.zeros((W_GROUP, W_GROUP), pool_w_l.dtype)
    for g in range(len(POOL_WINDOWS)):
        out = out.at[g * 64:(g + 1) * 64, g * 64:(g + 1) * 64].set(pool_w_l[g])
    return out


def _tokens_minor(a):
    lead = a.shape[:-3]
    n = len(lead)
    perm = tuple(range(n)) + (n + 1, n + 2, n)
    return a.transpose(perm).reshape(lead + (W_GROUP, a.shape[-3]))


def _tokens_major(a_t):
    lead = a_t.shape[:-2]
    n = len(lead)
    perm = tuple(range(n)) + (n + 2, n, n + 1)
    return a_t.reshape(lead + (N_HEADS, HEAD_DIM, a_t.shape[-1])).transpose(perm)


def kernel(x_prompt, x_sample, cache_k, cache_v, page_table, state_conv, state_pool, cache_mem_k, cache_mem_v,
           mem_prompt, g_norm, w_in, w_out, conv_w, conv_b, pool_w, pool_scale, g_q, g_k, g_mq, g_mk, g_mem,
           w_mem_kv):
    depth = w_in.shape[0]
    bp, seq, _ = x_prompt.shape
    bs, ts, _ = x_sample.shape
    n_pages = page_table.shape[1]
    past_len = n_pages * PAGE_SIZE
    bd = _same_head_matrix()
    tile4 = lambda g: jnp.tile(g, (1, N_HEADS))[:, None, :]

    w_in_bf = jnp.concatenate([w_in[:, :, g * W_GROUP:(g + 1) * W_GROUP].astype(BF16) for g in SPLIT_ORDER], axis=2)
    w_out_bf = w_out.astype(BF16)
    g_q_t, g_k_t, g_mq_t, g_mk_t = tile4(g_q), tile4(g_k), tile4(g_mq), tile4(g_mk)
    layer_w = []
    for l in range(depth):
        layer_w.append(dict(
            g_norm=g_norm[l][None, :], w_in=w_in_bf[l], conv_w=conv_w[l], conv_b=conv_b[l][None, :],
            pool_w=_pool_block_diag(pool_w[l]).astype(BF16), pool_scale=pool_scale[l][None, :],
            g_q=g_q_t[l], g_k=g_k_t[l], g_mq=g_mq_t[l], bd=bd))

    mkt_all, mvt_all, mk_bf, mvaug = _memkv(mem_prompt, g_mem[:, None, :], w_mem_kv.astype(BF16), g_mk_t, bd)

    cache_kt = _tokens_minor(cache_k)
    cache_vt = _tokens_minor(cache_v)
    memkt_s = _tokens_minor(cache_mem_k)
    memvt_s = _tokens_minor(cache_mem_v)

    xp = x_prompt
    xs = x_sample.reshape(bs * ts, D_MODEL)
    ks_l, vs_l, cp_l, cs_l, pp_l, ps_l = ([] for _ in range(6))
    kt, vt = None, None
    for l in range(depth):
        lw = layer_w[l]
        (q, kt, vt, kaug, vaug, kmean, yconv, ypool, ymem, az, cst, pst) = _prompt_front(
            xp, lw, mk_bf[l], mvaug[l], kt, vt)
        xp = _prompt_moba(q, kaug, vaug, kmean.reshape(bp, seq // ROW_TILE, W_GROUP), az,
                          xp, yconv, ypool, ymem, w_out_bf[l])
        cp_l.append(cst)
        pp_l.append(pst)
        (qs, ksn, vsn, mqs, azs, mzs, yconv_s, ypool_s, cst_s, pst_s) = _sample_front(
            xs, lw, state_conv[l], state_pool[l], past_len)
        yatt_s, ymem_s = _sample_moba(l, page_table, qs, ksn, vsn, mqs, azs, mzs, memkt_s, memvt_s,
                                      cache_kt, cache_vt)
        xs = _out_proj(xs, yconv_s, ypool_s, yatt_s, ymem_s, w_out_bf[l])
        ks_l.append(ksn.reshape(bs, ts, N_HEADS, HEAD_DIM))
        vs_l.append(vsn.reshape(bs, ts, N_HEADS, HEAD_DIM))
        cs_l.append(cst_s)
        ps_l.append(pst_s)

    return (xp, xs.reshape(bs, ts, D_MODEL), _tokens_major(kt), _tokens_major(vt), jnp.stack(ks_l), jnp.stack(vs_l),
            jnp.stack(cp_l), jnp.stack(cs_l), jnp.stack(pp_l), jnp.stack(ps_l),
            _tokens_major(mkt_all), _tokens_major(mvt_all))
```
